```python
import jax
import jax.numpy as jnp
from jax import lax
import numpy as np

D_MODEL = 4096
BATCH = 32
SEQ = 256
DEPTH = 1
DEC_BATCH = 4
DEC_SEQ = 1024
PAST_LEN = 512

GRID_W = 64
RET_HEADS = 8
RET_DK = D_MODEL // 16
RET_DV = D_MODEL // 16
RET_CHUNK = 128
ATT_HEADS = 16
ATT_KV_HEADS = 4
ATT_GROUPS = ATT_HEADS // ATT_KV_HEADS
ATT_HEAD_DIM = D_MODEL // 32
WINDOW = 128
BLOCK = 128
FFN_HIDDEN = ((8 * D_MODEL + 3 * 256 - 1) // (3 * 256)) * 256
ROPE_BASE = 10000.0
NORM_EPS = 1e-6
NEG_INF = -1e30
RET_QK_W = RET_HEADS * RET_DK
RET_V_W = RET_HEADS * RET_DV
ATT_Q_W = ATT_HEADS * ATT_HEAD_DIM
ATT_KV_W = ATT_KV_HEADS * ATT_HEAD_DIM
IN_SIZES = (RET_QK_W, RET_QK_W, RET_V_W, RET_V_W, ATT_Q_W, ATT_KV_W, ATT_KV_W, D_MODEL, D_MODEL)
IN_COLS = sum(IN_SIZES)

kernel_name = 'hybrid_retention_swa_diffusion_step'


def rmsnorm(x, w):
    x32 = x.astype(jnp.float32)
    y = x32 * lax.rsqrt(jnp.mean(x32 * x32, axis=-1, keepdims=True) + NORM_EPS)
    return (y * w.astype(jnp.float32)).astype(x.dtype)


def adaln_params(cond, w_mod, b_mod):
    return jnp.split(jax.nn.silu(cond) @ w_mod + b_mod, 6, axis=-1)


def modulate(x, shift, scale):
    return x * (1 + scale) + shift


def axial_rope(x):
    length, hd = x.shape[1], x.shape[-1]
    rows = length // GRID_W
    row = jnp.repeat(jnp.arange(rows), GRID_W).astype(jnp.float32)
    col = jnp.tile(jnp.arange(GRID_W), rows).astype(jnp.float32)
    half = hd // 2
    quarter = half // 2
    inv_freq = ROPE_BASE ** (-jnp.arange(quarter, dtype=jnp.float32) / quarter)

    def rotate(xa, pos):
        ang = pos[:, None] * inv_freq[None, :]
        cos = jnp.cos(ang)[None, :, None, :]
        sin = jnp.sin(ang)[None, :, None, :]
        x1, x2 = jnp.split(xa.astype(jnp.float32), 2, axis=-1)
        return jnp.concatenate([x1 * cos - x2 * sin, x1 * sin + x2 * cos], axis=-1)

    return jnp.concatenate([rotate(x[..., :half], row), rotate(x[..., half:], col)], axis=-1).astype(x.dtype)


def in_projection(h, w_in):
    proj = h @ w_in
    parts = []
    off = 0
    for size in IN_SIZES:
        parts.append(proj[..., off:off + size])
        off += size
    return parts


def swiglu(h, w_up, w_down):
    a, b = jnp.split(h @ w_up, 2, axis=-1)
    return (jax.nn.silu(a) * b) @ w_down


def retention_scan(q, k, v, log_gamma, s0):
    bsz, length, heads, _ = q.shape
    n_chunks = length // RET_CHUNK

    def chunks(a):
        return a.astype(jnp.float32).reshape(bsz, n_chunks, RET_CHUNK, heads, a.shape[-1]).transpose(1, 0, 3, 2, 4)

    idx = jnp.arange(RET_CHUNK, dtype=jnp.float32)
    diff = idx[:, None] - idx[None, :]
    d_intra = jnp.where(diff >= 0, jnp.exp(log_gamma[:, None, None] * jnp.maximum(diff, 0.0)), 0.0)
    d_q = jnp.exp(log_gamma[:, None] * (idx + 1.0))[:, :, None]
    d_k = jnp.exp(log_gamma[:, None] * (RET_CHUNK - 1.0 - idx))[:, :, None]
    d_chunk = jnp.exp(log_gamma * RET_CHUNK)[:, None, None]

    def step(state, qkv):
        qc, kc, vc = qkv
        scores = jnp.einsum('bhik,bhjk->bhij', qc, kc) * d_intra
        o = jnp.einsum('bhij,bhjv->bhiv', scores, vc) + jnp.einsum('bhik,bhkv->bhiv', qc * d_q, state)
        state = state * d_chunk + jnp.einsum('bhjk,bhjv->bhkv', kc * d_k, vc)
        return state, o

    s_final, out = lax.scan(step, s0.astype(jnp.float32), (chunks(q), chunks(k), chunks(v)))
    out = out.transpose(1, 0, 3, 2, 4).reshape(bsz, length, heads, v.shape[-1])
    return out, s_final


def retention_branch(rq, rk, rv, rg, log_decay_fwd, log_decay_bwd, gn_w, s0_fwd, s0_bwd, use_rope):
    bsz, length = rq.shape[:2]
    q = rq.reshape(bsz, length, RET_HEADS, RET_DK)
    k = rk.reshape(bsz, length, RET_HEADS, RET_DK) * (RET_DK ** -0.5)
    v = rv.reshape(bsz, length, RET_HEADS, RET_DV)
    if use_rope:
        q = axial_rope(q)
        k = axial_rope(k)
    log_gamma_f = -jnp.exp(log_decay_fwd.astype(jnp.float32))
    log_gamma_b = -jnp.exp(log_decay_bwd.astype(jnp.float32))
    o_f, s_f = retention_scan(q, k, v, log_gamma_f, s0_fwd)
    o_b, s_b = retention_scan(q[:, ::-1], k[:, ::-1], v[:, ::-1], log_gamma_b, s0_bwd)
    o = o_f + o_b[:, ::-1]
    o = o * lax.rsqrt(jnp.mean(o * o, axis=-1, keepdims=True) + NORM_EPS)
    o = o.reshape(bsz, length, RET_V_W) * gn_w.astype(jnp.float32)
    return jax.nn.silu(rg) * o.astype(rg.dtype), s_f, s_b


def sink_attention(qb, parts, sink):
    bsz, nq = qb.shape[:2]
    scores = []
    for k, v, mask in parts:
        s = jnp.einsum('bqhgd,bkhd->bhgqk', qb, k).astype(jnp.float32)
        if mask is not None:
            s = jnp.where(mask, s, NEG_INF)
        scores.append(s)
    sink_col = jnp.broadcast_to(sink.astype(jnp.float32).reshape(ATT_KV_HEADS, ATT_GROUPS, 1, 1), (bsz, ATT_KV_HEADS, ATT_GROUPS, nq, 1))
    probs = jax.nn.softmax(jnp.concatenate(scores + [sink_col], axis=-1), axis=-1)
    out = None
    off = 0
    for k, v, mask in parts:
        n_keys = k.shape[1]
        o = jnp.einsum('bhgqk,bkhd->bqhgd', probs[..., off:off + n_keys].astype(v.dtype), v)
        out = o if out is None else out + o
        off += n_keys
    return out


def context_attention(q, k, v, sink):
    bsz, length = q.shape[:2]
    n_blocks = length // BLOCK

    def block(n):
        qb = lax.dynamic_slice_in_dim(q, n * BLOCK, BLOCK, axis=1)
        return sink_attention(qb, [(k, v, None)], sink)

    out = lax.map(block, jnp.arange(n_blocks))
    return jnp.moveaxis(out, 0, 1).reshape(bsz, length, ATT_Q_W)


def windowed_attention(q, k, v, k_ctx, v_ctx, sink):
    bsz, length = q.shape[:2]
    n_blocks = length // BLOCK
    pad = ((0, 0), (BLOCK, BLOCK), (0, 0), (0, 0))
    kp = jnp.pad(k, pad)
    vp = jnp.pad(v, pad)
    qi = jnp.arange(BLOCK)
    kj = jnp.arange(3 * BLOCK)

    def block(n):
        qb = lax.dynamic_slice_in_dim(q, n * BLOCK, BLOCK, axis=1)
        kw = lax.dynamic_slice_in_dim(kp, n * BLOCK, 3 * BLOCK, axis=1)
        vw = lax.dynamic_slice_in_dim(vp, n * BLOCK, 3 * BLOCK, axis=1)
        qpos = n * BLOCK + qi
        kpos = (n - 1) * BLOCK + kj
        mask = (jnp.abs(kpos[None, :] - qpos[:, None]) <= WINDOW) & (kpos >= 0)[None, :] & (kpos < length)[None, :]
        return sink_attention(qb, [(kw, vw, mask), (k_ctx, v_ctx, None)], sink)

    out = lax.map(block, jnp.arange(n_blocks))
    return jnp.moveaxis(out, 0, 1).reshape(bsz, length, ATT_Q_W)


def merge_branches(ret_out, att_out, gr, ga, w_branch_ret, w_branch_att, w_out):
    mix = jax.nn.sigmoid(gr) * (ret_out @ w_branch_ret) + jax.nn.sigmoid(ga) * (att_out @ w_branch_att)
    return mix @ w_out


def finish_layer(x, mix, mods, lw):
    x = x + mods[2] * rmsnorm(mix, lw['norm_mix_post'])
    h = modulate(rmsnorm(x, lw['norm_ffn_pre']), mods[3], mods[4])
    return x + mods[5] * rmsnorm(swiglu(h, lw['w_ffn_up'], lw['w_ffn_down']), lw['norm_ffn_post'])


def context_layer(x, mods, lw):
    bsz, length = x.shape[:2]
    h = modulate(rmsnorm(x, lw['norm_mix_pre']), mods[0], mods[1])
    rq, rk, rv, rg, aq, ak, av, gr, ga = in_projection(h, lw['w_in'])
    zeros = jnp.zeros((bsz, RET_HEADS, RET_DK, RET_DV), jnp.float32)
    ret_out, s_f, s_b = retention_branch(rq, rk, rv, rg, lw['ret_log_decay_fwd'], lw['ret_log_decay_bwd'], lw['ret_gn_w'], zeros, zeros, False)
    q = aq.reshape(bsz, length, ATT_KV_HEADS, ATT_GROUPS, ATT_HEAD_DIM) * (ATT_HEAD_DIM ** -0.5)
    k = ak.reshape(bsz, length, ATT_KV_HEADS, ATT_HEAD_DIM)
    v = av.reshape(bsz, length, ATT_KV_HEADS, ATT_HEAD_DIM)
    att_out = context_attention(q, k, v, lw['att_sink'])
    mix = merge_branches(ret_out, att_out, gr, ga, lw['w_branch_ret'], lw['w_branch_att'], lw['w_out'])
    return finish_layer(x, mix, mods, lw), k, v, s_f, s_b


def latent_layer(x, mods, k_ctx, v_ctx, s_ctx_fwd, s_ctx_bwd, lw):
    bsz, length = x.shape[:2]
    h = modulate(rmsnorm(x, lw['norm_mix_pre']), mods[0], mods[1])
    rq, rk, rv, rg, aq, ak, av, gr, ga = in_projection(h, lw['w_in'])
    ret_out, _, _ = retention_branch(rq, rk, rv, rg, lw['ret_log_decay_fwd'], lw['ret_log_decay_bwd'], lw['ret_gn_w'], s_ctx_fwd, s_ctx_bwd, True)
    q = axial_rope(aq.reshape(bsz, length, ATT_HEADS, ATT_HEAD_DIM)) * (ATT_HEAD_DIM ** -0.5)
    q = q.reshape(bsz, length, ATT_KV_HEADS, ATT_GROUPS, ATT_HEAD_DIM)
    k = axial_rope(ak.reshape(bsz, length, ATT_KV_HEADS, ATT_HEAD_DIM))
    v = av.reshape(bsz, length, ATT_KV_HEADS, ATT_HEAD_DIM)
    att_out = windowed_attention(q, k, v, k_ctx.astype(k.dtype), v_ctx.astype(v.dtype), lw['att_sink'])
    mix = merge_branches(ret_out, att_out, gr, ga, lw['w_branch_ret'], lw['w_branch_att'], lw['w_out'])
    return finish_layer(x, mix, mods, lw)


def setup_inputs(seed: int = 0) -> dict:
    key = jax.random.key(seed)
    ks = jax.random.split(key, 24)
    f32 = jnp.float32

    def nrm(k, shape, scale=1.0):
        return jax.random.normal(k, shape, f32) * scale

    decay_base = jnp.asarray(np.log(-np.log1p(-(2.0 ** (-5.0 - np.arange(RET_HEADS))))), f32)
    return {
        'x_prompt': nrm(ks[0], (BATCH, SEQ, D_MODEL)),
        'x_sample': nrm(ks[1], (DEC_BATCH, DEC_SEQ, D_MODEL)),
        'cache_att_k': nrm(ks[2], (DEC_BATCH, DEPTH, PAST_LEN, ATT_KV_HEADS, ATT_HEAD_DIM)),
        'cache_att_v': nrm(ks[3], (DEC_BATCH, DEPTH, PAST_LEN, ATT_KV_HEADS, ATT_HEAD_DIM)),
        'state_ret_fwd': nrm(ks[4], (DEC_BATCH, DEPTH, RET_HEADS, RET_DK, RET_DV), 0.5),
        'state_ret_bwd': nrm(ks[5], (DEC_BATCH, DEPTH, RET_HEADS, RET_DK, RET_DV), 0.5),
        'c': nrm(ks[6], (DEC_BATCH, D_MODEL)),
        'c_ctx': nrm(ks[7], (D_MODEL,)),
        'w_mod': nrm(ks[8], (DEPTH, D_MODEL, 6 * D_MODEL), D_MODEL ** -0.5),
        'b_mod': nrm(ks[9], (DEPTH, 6 * D_MODEL), 0.01),
        'norm_mix_pre': 1.0 + nrm(ks[10], (DEPTH, D_MODEL), 0.05),
        'norm_mix_post': 1.0 + nrm(ks[11], (DEPTH, D_MODEL), 0.05),
        'norm_ffn_pre': 1.0 + nrm(ks[12], (DEPTH, D_MODEL), 0.05),
        'norm_ffn_post': 1.0 + nrm(ks[13], (DEPTH, D_MODEL), 0.05),
        'w_in': nrm(ks[14], (DEPTH, D_MODEL, IN_COLS), D_MODEL ** -0.5),
        'ret_log_decay_fwd': decay_base + nrm(ks[15], (DEPTH, RET_HEADS), 0.05),
        'ret_log_decay_bwd': decay_base + nrm(ks[16], (DEPTH, RET_HEADS), 0.05),
        'ret_gn_w': 1.0 + nrm(ks[17], (DEPTH, RET_V_W), 0.05),
        'att_sink': nrm(ks[18], (DEPTH, ATT_HEADS), 0.5),
        'w_branch_ret': nrm(ks[19], (DEPTH, RET_V_W, D_MODEL), RET_V_W ** -0.5),
        'w_branch_att': nrm(ks[20], (DEPTH, ATT_Q_W, D_MODEL), ATT_Q_W ** -0.5),
        'w_out': nrm(ks[21], (DEPTH, D_MODEL, D_MODEL), D_MODEL ** -0.5),
        'w_ffn_up': nrm(ks[22], (DEPTH, D_MODEL, 2 * FFN_HIDDEN), D_MODEL ** -0.5),
        'w_ffn_down': nrm(ks[23], (DEPTH, FFN_HIDDEN, D_MODEL), FFN_HIDDEN ** -0.5),
    }


def reference(x_prompt, x_sample, cache_att_k, cache_att_v, state_ret_fwd, state_ret_bwd, c, c_ctx,
              w_mod, b_mod, norm_mix_pre, norm_mix_post, norm_ffn_pre, norm_ffn_post, w_in,
              ret_log_decay_fwd, ret_log_decay_bwd, ret_gn_w, att_sink, w_branch_ret, w_branch_att,
              w_out, w_ffn_up, w_ffn_down):
    y_prompt = x_prompt
    y_sample = x_sample
    ks_out, vs_out, sf_out, sb_out = [], [], [], []
    for l in range(DEPTH):
        lw = {
            'norm_mix_pre': norm_mix_pre[l], 'norm_mix_post': norm_mix_post[l],
            'norm_ffn_pre': norm_ffn_pre[l], 'norm_ffn_post': norm_ffn_post[l],
            'w_in': w_in[l], 'ret_log_decay_fwd': ret_log_decay_fwd[l], 'ret_log_decay_bwd': ret_log_decay_bwd[l],
            'ret_gn_w': ret_gn_w[l], 'att_sink': att_sink[l], 'w_branch_ret': w_branch_ret[l],
            'w_branch_att': w_branch_att[l], 'w_out': w_out[l], 'w_ffn_up': w_ffn_up[l], 'w_ffn_down': w_ffn_down[l],
        }
        mods_ctx = adaln_params(c_ctx, w_mod[l], b_mod[l])
        mods_lat = [m[:, None, :] for m in adaln_params(c, w_mod[l], b_mod[l])]
        y_prompt, k_l, v_l, sf_l, sb_l = context_layer(y_prompt, mods_ctx, lw)
        ks_out.append(k_l)
        vs_out.append(v_l)
        sf_out.append(sf_l)
        sb_out.append(sb_l)
        y_sample = latent_layer(y_sample, mods_lat, cache_att_k[:, l], cache_att_v[:, l],
                                state_ret_fwd[:, l], state_ret_bwd[:, l], lw)
    new_att_k = jnp.stack(ks_out, axis=1)
    new_att_v = jnp.stack(vs_out, axis=1)
    new_ret_fwd = jnp.stack(sf_out, axis=1)
    new_ret_bwd = jnp.stack(sb_out, axis=1)
    return (y_prompt, y_sample, new_att_k, new_att_v, new_ret_fwd, new_ret_bwd)
```

```python
import functools

import jax
import jax.numpy as jnp
from jax import lax
from jax.experimental import pallas as pl
from jax.experimental.pallas import tpu as pltpu

F32 = jnp.float32
BF16 = jnp.bfloat16

GRID_W = 64
RET_HEADS = 8
ATT_HEADS = 16
ATT_KV_HEADS = 4
ATT_GROUPS = ATT_HEADS // ATT_KV_HEADS
WINDOW = 128
BLOCK = 128
ROPE_BASE = 10000.0
NORM_EPS = 1e-6
NEG_INF = -1e30

VMEM_LIMIT_BYTES = 56 * 1024 * 1024
LANES = 128


def _params(n_axes):
    return pltpu.CompilerParams(
        dimension_semantics=("arbitrary",) * n_axes, vmem_limit_bytes=VMEM_LIMIT_BYTES)


def _rms_scale(x):
    return lax.rsqrt(jnp.mean(x * x, axis=-1, keepdims=True) + NORM_EPS)


def _dot(a, b):
    return jnp.dot(a, b, preferred_element_type=F32)


def _dot_nt(a, b):
    return lax.dot_general(a, b, (((1,), (1,)), ((), ())), preferred_element_type=F32)


def _adaln_body(c_ref, w_ref, b_ref, o_ref):
    o_ref[...] = _dot(jax.nn.silu(c_ref[...]), w_ref[...]) + b_ref[...]


def _adaln(cond, w_mod, b_mod, tn=512):
    rows, d = cond.shape
    n = w_mod.shape[1]
    return pl.pallas_call(
        _adaln_body,
        out_shape=jax.ShapeDtypeStruct((rows, n), F32),
        grid=(n // tn,),
        in_specs=[pl.BlockSpec((rows, d), lambda j: (0, 0)),
                  pl.BlockSpec((d, tn), lambda j: (0, j)),
                  pl.BlockSpec((1, tn), lambda j: (0, j))],
        out_specs=pl.BlockSpec((rows, tn), lambda j: (0, j)),
        compiler_params=_params(1),
        name="adaln",
    )(cond, w_mod, b_mod)


def _mods_spec(d, m, group_of):
    return pl.BlockSpec((None, None, 1, d), lambda i: (group_of(i), m, 0, 0))


def _prenorm_body(nc_tiles, xp_ref, xs_ref, w_ref, sh_ref, sc_ref, h_ref):
    def run(x_ref):
        x = x_ref[...]
        y = x * _rms_scale(x) * w_ref[...]
        h_ref[...] = (y * (1.0 + sc_ref[...]) + sh_ref[...]).astype(h_ref.dtype)

    i = pl.program_id(0)
    pl.when(i < nc_tiles)(lambda: run(xp_ref))
    pl.when(i >= nc_tiles)(lambda: run(xs_ref))


def _post1_body(nc_tiles, xp_ref, xs_ref, mix_ref, g1_ref, wpost_ref, wpre_ref, sh_ref, sc_ref,
                x1_ref, h2_ref):
    def run(x_ref):
        m = mix_ref[...]
        x1 = x_ref[...] + g1_ref[...] * (m * _rms_scale(m) * wpost_ref[...])
        x1_ref[...] = x1
        y = x1 * _rms_scale(x1) * wpre_ref[...]
        h2_ref[...] = (y * (1.0 + sc_ref[...]) + sh_ref[...]).astype(h2_ref.dtype)

    i = pl.program_id(0)
    pl.when(i < nc_tiles)(lambda: run(xp_ref))
    pl.when(i >= nc_tiles)(lambda: run(xs_ref))


def _final_body(x1_ref, f_ref, g2_ref, w_ref, y_ref):
    f = f_ref[...]
    y_ref[...] = x1_ref[...] + g2_ref[...] * (f * _rms_scale(f) * w_ref[...])


def _mm_body(x_ref, w_ref, o_ref):
    o_ref[...] = _dot(x_ref[...], w_ref[...].astype(BF16)).astype(o_ref.dtype)


def _matmul(x, w, *, n_out, w_col_block, out_dtype, tm, tn, name):
    m, k = x.shape
    return pl.pallas_call(
        _mm_body,
        out_shape=jax.ShapeDtypeStruct((m, n_out), out_dtype),
        grid=(n_out // tn, m // tm),
        in_specs=[pl.BlockSpec((tm, k), lambda j, i: (i, 0)),
                  pl.BlockSpec((k, tn), lambda j, i: (0, w_col_block(j)))],
        out_specs=pl.BlockSpec((tm, tn), lambda j, i: (i, j)),
        compiler_params=_params(2),
        name=name,
    )(x, w)


def _merge_body(r_ref, a_ref, wr_ref, wa_ref, gr_ref, ga_ref, o_ref):
    r = _dot(r_ref[...], wr_ref[...].astype(BF16))
    a = _dot(a_ref[...], wa_ref[...].astype(BF16))
    gr = jax.nn.sigmoid(gr_ref[...].astype(F32))
    ga = jax.nn.sigmoid(ga_ref[...].astype(F32))
    o_ref[...] = (gr * r + ga * a).astype(o_ref.dtype)


def _ffn_up_body(h_ref, wa_ref, wb_ref, o_ref):
    h = h_ref[...]
    a = _dot(h, wa_ref[...].astype(BF16))
    b = _dot(h, wb_ref[...].astype(BF16))
    o_ref[...] = (jax.nn.silu(a) * b).astype(o_ref.dtype)


def _cast_body(x_ref, o_ref):
    o_ref[...] = x_ref[...].astype(o_ref.dtype)


def _decay_body(lc, ldf_ref, ldb_ref, d_ref, qf_ref, qb_ref, kf_ref, kb_ref):
    h = pl.program_id(0)
    r = pl.program_id(1)
    rb, l = d_ref.shape

    def log_gamma(ld_ref, shape):
        return -jnp.exp(jnp.full(shape, ld_ref[h], F32))

    def rows(shape):
        return (lax.broadcasted_iota(jnp.int32, shape, 0) + r * rb).astype(F32)

    def cols(shape):
        return lax.broadcasted_iota(jnp.int32, shape, 1).astype(F32)

    diff = rows((rb, l)) - cols((rb, l))
    d_f = jnp.where(diff >= 0, jnp.exp(log_gamma(ldf_ref, (rb, l)) * jnp.maximum(diff, 0.0)), 0.0)
    d_b = jnp.where(diff <= 0, jnp.exp(log_gamma(ldb_ref, (rb, l)) * jnp.maximum(-diff, 0.0)), 0.0)
    d_ref[...] = d_f + d_b
    qs = qf_ref.shape
    qf_ref[...] = jnp.exp(log_gamma(ldf_ref, qs) * (rows(qs) + 1.0))
    qb_ref[...] = jnp.exp(log_gamma(ldb_ref, qs) * (float(l) - rows(qs)))
    ks = kf_ref.shape
    kf_ref[...] = jnp.exp(log_gamma(ldf_ref, ks) * (float(lc - 1) - cols(ks)))
    kb_ref[...] = jnp.exp(log_gamma(ldb_ref, ks) * cols(ks))


def _decay_tables(ldf, ldb, l, lc, dk):
    rb = 256
    return pl.pallas_call(
        functools.partial(_decay_body, lc),
        out_shape=(jax.ShapeDtypeStruct((RET_HEADS, l, l), F32),
                   jax.ShapeDtypeStruct((RET_HEADS, l, dk), F32),
                   jax.ShapeDtypeStruct((RET_HEADS, l, dk), F32),
                   jax.ShapeDtypeStruct((RET_HEADS, 8, lc), F32),
                   jax.ShapeDtypeStruct((RET_HEADS, 8, lc), F32)),
        grid=(RET_HEADS, l // rb),
        in_specs=[pl.BlockSpec(memory_space=pltpu.SMEM), pl.BlockSpec(memory_space=pltpu.SMEM)],
        out_specs=(pl.BlockSpec((None, rb, l), lambda h, r: (h, r, 0)),
                   pl.BlockSpec((None, rb, dk), lambda h, r: (h, r, 0)),
                   pl.BlockSpec((None, rb, dk), lambda h, r: (h, r, 0)),
                   pl.BlockSpec((None, 8, lc), lambda h, r: (h, 0, 0)),
                   pl.BlockSpec((None, 8, lc), lambda h, r: (h, 0, 0))),
        compiler_params=_params(2),
        name="decay_tables",
    )(ldf, ldb)


def _gated_group_norm(o, g_bf16, gn):
    on = o * _rms_scale(o) * gn
    return jax.nn.silu(g_bf16.astype(F32)) * on


def _ret_ctx_body(dk, q_ref, k_ref, v_ref, g_ref, d_ref, kf_ref, kb_ref, gn_ref, o_ref, sf_ref, sb_ref):
    for h in range(RET_HEADS):
        sl = slice(h * dk, (h + 1) * dk)
        q = q_ref[:, sl]
        k32 = k_ref[:, sl].astype(F32) * (dk ** -0.5)
        v = v_ref[:, sl]
        p = (_dot_nt(q, k32.astype(BF16)) * d_ref[h]).astype(BF16)
        o = _dot(p, v)
        kt = k32.T
        sf_ref[h] = _dot((kt * kf_ref[h][0:1, :]).astype(BF16), v)
        sb_ref[h] = _dot((kt * kb_ref[h][0:1, :]).astype(BF16), v)
        o_ref[:, sl] = _gated_group_norm(o, g_ref[:, sl], gn_ref[:, sl]).astype(o_ref.dtype)


def _ret_lat_body(dk, rb, buf_ref, q_ref, k_ref, v_ref, g_ref, d_ref, qf_ref, qb_ref, cos_ref, sin_ref,
                  s0f_ref, s0b_ref, gn_ref, o_ref):
    del buf_ref
    cos = cos_ref[...]
    sin = sin_ref[...]

    def rope(x):
        swapped = jnp.concatenate(
            [pltpu.roll(x[:, c * LANES:(c + 1) * LANES], LANES // 2, 1) for c in range(dk // LANES)], axis=1)
        return x * cos + swapped * sin

    q = rope(q_ref[...].astype(F32)).astype(BF16)
    k = rope(k_ref[...].astype(F32) * (dk ** -0.5)).astype(BF16)
    v = v_ref[...]
    s0f = s0f_ref[...].astype(BF16)
    s0b = s0b_ref[...].astype(BF16)
    gn = gn_ref[...]
    for r in range(q.shape[0] // rb):
        rs = slice(r * rb, (r + 1) * rb)
        qr = q[rs]
        p = (_dot_nt(qr, k) * d_ref[rs, :]).astype(BF16)
        o = _dot(p, v) + qf_ref[rs, :] * _dot(qr, s0f) + qb_ref[rs, :] * _dot(qr, s0b)
        o_ref[rs, :] = _gated_group_norm(o, g_ref[rs, :], gn).astype(o_ref.dtype)


def _softmax_parts(scores, sink_col):
    m = sink_col
    for s in scores:
        m = jnp.maximum(m, jnp.max(s, axis=-1, keepdims=True))
    ps = [jnp.exp(s - m) for s in scores]
    denom = jnp.exp(sink_col - m)
    for p in ps:
        denom = denom + jnp.sum(p, axis=-1, keepdims=True)
    return ps, denom


def _sink_col(sink_ref, kh, rows):
    return jnp.concatenate(
        [jnp.full((rows, 1), sink_ref[kh * ATT_GROUPS + g], F32) for g in range(ATT_GROUPS)], axis=0)


def _att_ctx_body(hd, sink_ref, q_ref, k_ref, v_ref, o_ref):
    rows = q_ref.shape[0]
    scale = hd ** -0.5
    for kh in range(ATT_KV_HEADS):
        cs = slice(kh * hd, (kh + 1) * hd)
        heads = [kh * ATT_GROUPS + g for g in range(ATT_GROUPS)]
        qg = jnp.concatenate([q_ref[:, hh * hd:(hh + 1) * hd] for hh in heads], axis=0)
        s = _dot_nt(qg, k_ref[:, cs].astype(BF16)) * scale
        (p,), denom = _softmax_parts([s], _sink_col(sink_ref, kh, rows))
        o = _dot(p.astype(BF16), v_ref[:, cs].astype(BF16)) / denom
        for g, hh in enumerate(heads):
            o_ref[:, hh * hd:(hh + 1) * hd] = o[g * rows:(g + 1) * rows].astype(o_ref.dtype)


def _att_lat_body(hd, length, sink_ref, buf_ref, q_ref, kp_ref, kc_ref, kn_ref, vp_ref, vc_ref, vn_ref,
                  ck_ref, cv_ref, cos_ref, sa_ref, sb_ref, o_ref):
    del buf_ref
    n = pl.program_id(1)
    nb = length // BLOCK
    scale = hd ** -0.5

    def rope(x, blk):
        rows = pl.ds(pl.multiple_of(blk * BLOCK, BLOCK), BLOCK)
        return (x * cos_ref[rows, :] + pltpu.roll(x, 3 * hd // 4, 1) * sa_ref[rows, :]
                + pltpu.roll(x, hd // 4, 1) * sb_ref[rows, :])

    bp = jnp.maximum(n - 1, 0)
    bn = jnp.minimum(n + 1, nb - 1)
    qpos = n * BLOCK + lax.broadcasted_iota(jnp.int32, (BLOCK, 3 * BLOCK), 0)
    kpos = (n - 1) * BLOCK + lax.broadcasted_iota(jnp.int32, (BLOCK, 3 * BLOCK), 1)
    mask = (jnp.abs(kpos - qpos) <= WINDOW) & (kpos >= 0) & (kpos < length)
    mask = jnp.concatenate([mask] * ATT_GROUPS, axis=0)
    for kh in range(ATT_KV_HEADS):
        cs = slice(kh * hd, (kh + 1) * hd)
        heads = [kh * ATT_GROUPS + g for g in range(ATT_GROUPS)]
        kw = jnp.concatenate(
            [rope(kp_ref[:, cs], bp), rope(kc_ref[:, cs], n), rope(kn_ref[:, cs], bn)], axis=0).astype(BF16)
        vw = jnp.concatenate([vp_ref[:, cs], vc_ref[:, cs], vn_ref[:, cs]], axis=0).astype(BF16)
        qg = jnp.concatenate(
            [rope(q_ref[:, hh * hd:(hh + 1) * hd].astype(F32), n) for hh in heads], axis=0).astype(BF16)
        sw = jnp.where(mask, _dot_nt(qg, kw) * scale, NEG_INF)
        sc = _dot_nt(qg, ck_ref[:, cs].astype(BF16)) * scale
        (pw, pc), denom = _softmax_parts([sw, sc], _sink_col(sink_ref, kh, BLOCK))
        o = (_dot(pw.astype(BF16), vw) + _dot(pc.astype(BF16), cv_ref[:, cs].astype(BF16))) / denom
        for g, hh in enumerate(heads):
            o_ref[:, hh * hd:(hh + 1) * hd] = o[g * BLOCK:(g + 1) * BLOCK].astype(o_ref.dtype)


def _rope_tables(length, hd):
    half = hd // 2
    quarter = half // 2
    rows = length // GRID_W
    row = jnp.repeat(jnp.arange(rows), GRID_W).astype(F32)
    col = jnp.tile(jnp.arange(GRID_W), rows).astype(F32)
    inv_freq = ROPE_BASE ** (-jnp.arange(quarter, dtype=F32) / quarter)
    ang_r = row[:, None] * inv_freq[None, :]
    ang_c = col[:, None] * inv_freq[None, :]
    zeros = jnp.zeros_like(ang_r)
    cos = jnp.concatenate([jnp.cos(ang_r)] * 2 + [jnp.cos(ang_c)] * 2, axis=-1)
    sin_hi = jnp.concatenate([-jnp.sin(ang_r), zeros, -jnp.sin(ang_c), zeros], axis=-1)
    sin_lo = jnp.concatenate([zeros, jnp.sin(ang_r), zeros, jnp.sin(ang_c)], axis=-1)
    return cos, sin_hi, sin_lo


def kernel(x_prompt, x_sample, cache_att_k, cache_att_v, state_ret_fwd, state_ret_bwd, c, c_ctx, w_mod, b_mod, norm_mix_pre, norm_mix_post, norm_ffn_pre, norm_ffn_post, w_in, ret_log_decay_fwd, ret_log_decay_bwd, ret_gn_w, att_sink, w_branch_ret, w_branch_att, w_out, w_ffn_up, w_ffn_down):
    nbc, lc, d = x_prompt.shape
    nbl, ll, _ = x_sample.shape
    depth = w_mod.shape[0]
    assert depth == 1
    tc = nbc * lc
    tl = nbl * ll
    t = tc + tl
    ret_w = w_branch_ret.shape[1]
    att_w = w_branch_att.shape[1]
    dk = ret_w // RET_HEADS
    hd = att_w // ATT_HEADS
    kv_w = ATT_KV_HEADS * hd
    hidden = w_ffn_down.shape[1]
    past = cache_att_k.shape[2]
    assert ret_w == att_w and ll % GRID_W == 0

    xp = x_prompt.reshape(tc, d)
    xs = x_sample.reshape(tl, d)
    w_in2 = w_in.reshape(d, -1)

    groups = 8
    assert 1 + nbl <= groups
    cond = jnp.concatenate([c_ctx[None, :], c, jnp.zeros((groups - 1 - nbl, d), F32)], axis=0)
    mods = _adaln(cond, w_mod.reshape(d, 6 * d), b_mod.reshape(1, 6 * d)).reshape(groups, 6, 1, d)

    def group_of(tr):
        return lambda i: jnp.where(i * tr < tc, 0, 1 + (i * tr - tc) // ll)

    def vec(a):
        return a.reshape(1, d)

    def row_spec(tr, width, off=0):
        return pl.BlockSpec((tr, width), lambda i: (i + off, 0))

    def x_specs(tr):
        nct = tc // tr
        return [pl.BlockSpec((tr, d), lambda i: (jnp.minimum(i, nct - 1), 0)),
                pl.BlockSpec((tr, d), lambda i: (jnp.maximum(i - nct, 0), 0))]

    vec_spec = pl.BlockSpec((1, d), lambda i: (0, 0))

    tr = 256
    g = group_of(tr)
    h = pl.pallas_call(
        functools.partial(_prenorm_body, tc // tr),
        out_shape=jax.ShapeDtypeStruct((t, d), BF16),
        grid=(t // tr,),
        in_specs=x_specs(tr) + [vec_spec, _mods_spec(d, 0, g), _mods_spec(d, 1, g)],
        out_specs=row_spec(tr, d),
        compiler_params=_params(1),
        name="prenorm",
    )(xp, xs, vec(norm_mix_pre), mods, mods)

    tm, tn = 1024, 512
    kv_off = 4 * ret_w + att_w
    gate_off = kv_off + 2 * kv_w
    n_front = kv_off // tn
    proj = _matmul(h, w_in2, n_out=kv_off + 2 * d,
                   w_col_block=lambda j: jnp.where(j < n_front, j, j + (gate_off - kv_off) // tn),
                   out_dtype=BF16, tm=tm, tn=tn, name="in_proj")
    kv = _matmul(h, w_in2, n_out=2 * kv_w, w_col_block=lambda j: j + kv_off // tn,
                 out_dtype=F32, tm=tm, tn=tn, name="in_proj_kv")

    d_tab, qf_tab, qb_tab, kf_tab, kb_tab = _decay_tables(
        ret_log_decay_fwd.reshape(RET_HEADS), ret_log_decay_bwd.reshape(RET_HEADS), ll, lc, dk)
    gn = ret_gn_w.reshape(1, ret_w)
    ret_ctx, s_f, s_b = pl.pallas_call(
        functools.partial(_ret_ctx_body, dk),
        out_shape=(jax.ShapeDtypeStruct((t, ret_w), BF16),
                   jax.ShapeDtypeStruct((nbc, 1, RET_HEADS, dk, dk), F32),
                   jax.ShapeDtypeStruct((nbc, 1, RET_HEADS, dk, dk), F32)),
        grid=(nbc,),
        in_specs=[pl.BlockSpec((lc, ret_w), lambda b: (b, 0)),
                  pl.BlockSpec((lc, ret_w), lambda b: (b, 1)),
                  pl.BlockSpec((lc, ret_w), lambda b: (b, 2)),
                  pl.BlockSpec((lc, ret_w), lambda b: (b, 3)),
                  pl.BlockSpec((RET_HEADS, lc, lc), lambda b: (0, 0, 0)),
                  pl.BlockSpec((RET_HEADS, 8, lc), lambda b: (0, 0, 0)),
                  pl.BlockSpec((RET_HEADS, 8, lc), lambda b: (0, 0, 0)),
                  pl.BlockSpec((1, ret_w), lambda b: (0, 0))],
        out_specs=(pl.BlockSpec((lc, ret_w), lambda b: (b, 0)),
                   pl.BlockSpec((None, None, RET_HEADS, dk, dk), lambda b: (b, 0, 0, 0, 0)),
                   pl.BlockSpec((None, None, RET_HEADS, dk, dk), lambda b: (b, 0, 0, 0, 0))),
        compiler_params=_params(1),
        name="ret_ctx",
    )(proj, proj, proj, proj, d_tab, kf_tab, kb_tab, gn)

    cos_r, sin_hi_r, sin_lo_r = _rope_tables(ll, dk)
    rb0 = tc // ll
    hb = ret_w // dk
    ret_out = pl.pallas_call(
        functools.partial(_ret_lat_body, dk, 256),
        out_shape=jax.ShapeDtypeStruct((t, ret_w), BF16),
        grid=(RET_HEADS, nbl),
        in_specs=[pl.BlockSpec(memory_space=pl.ANY),
                  pl.BlockSpec((ll, dk), lambda hh, b: (rb0 + b, hh)),
                  pl.BlockSpec((ll, dk), lambda hh, b: (rb0 + b, hb + hh)),
                  pl.BlockSpec((ll, dk), lambda hh, b: (rb0 + b, 2 * hb + hh)),
                  pl.BlockSpec((ll, dk), lambda hh, b: (rb0 + b, 3 * hb + hh)),
                  pl.BlockSpec((None, ll, ll), lambda hh, b: (hh, 0, 0)),
                  pl.BlockSpec((None, ll, dk), lambda hh, b: (hh, 0, 0)),
                  pl.BlockSpec((None, ll, dk), lambda hh, b: (hh, 0, 0)),
                  pl.BlockSpec((ll, dk), lambda hh, b: (0, 0)),
                  pl.BlockSpec((ll, dk), lambda hh, b: (0, 0)),
                  pl.BlockSpec((None, None, None, dk, dk), lambda hh, b: (b, 0, hh, 0, 0)),
                  pl.BlockSpec((None, None, None, dk, dk), lambda hh, b: (b, 0, hh, 0, 0)),
                  pl.BlockSpec((1, dk), lambda hh, b: (0, hh))],
        out_specs=pl.BlockSpec((ll, dk), lambda hh, b: (rb0 + b, hh)),
        input_output_aliases={0: 0},
        compiler_params=_params(2),
        name="ret_lat",
    )(ret_ctx, proj, proj, proj, proj, d_tab, qf_tab, qb_tab, cos_r, sin_hi_r + sin_lo_r,
      state_ret_fwd, state_ret_bwd, gn)

    sink = att_sink.reshape(ATT_HEADS)
    aq_blk = 4 * ret_w // att_w
    att_ctx = pl.pallas_call(
        functools.partial(_att_ctx_body, hd),
        out_shape=jax.ShapeDtypeStruct((t, att_w), BF16),
        grid=(nbc,),
        in_specs=[pl.BlockSpec(memory_space=pltpu.SMEM),
                  pl.BlockSpec((lc, att_w), lambda b: (b, aq_blk)),
                  pl.BlockSpec((lc, kv_w), lambda b: (b, 0)),
                  pl.BlockSpec((lc, kv_w), lambda b: (b, 1))],
        out_specs=pl.BlockSpec((lc, att_w), lambda b: (b, 0)),
        compiler_params=_params(1),
        name="att_ctx",
    )(sink, proj, kv, kv)

    cos_a, sin_hi_a, sin_lo_a = _rope_tables(ll, hd)
    nqb = ll // BLOCK
    qb0 = tc // BLOCK

    def kv_spec(shift, col):
        return pl.BlockSpec(
            (BLOCK, kv_w), lambda b, n: (qb0 + b * nqb + jnp.clip(n + shift, 0, nqb - 1), col))

    tab_spec = pl.BlockSpec((ll, hd), lambda b, n: (0, 0))
    att_out = pl.pallas_call(
        functools.partial(_att_lat_body, hd, ll),
        out_shape=jax.ShapeDtypeStruct((t, att_w), BF16),
        grid=(nbl, nqb),
        in_specs=[pl.BlockSpec(memory_space=pltpu.SMEM),
                  pl.BlockSpec(memory_space=pl.ANY),
                  pl.BlockSpec((BLOCK, att_w), lambda b, n: (qb0 + b * nqb + n, aq_blk)),
                  kv_spec(-1, 0), kv_spec(0, 0), kv_spec(1, 0),
                  kv_spec(-1, 1), kv_spec(0, 1), kv_spec(1, 1),
                  pl.BlockSpec((None, past, kv_w), lambda b, n: (b, 0, 0)),
                  pl.BlockSpec((None, past, kv_w), lambda b, n: (b, 0, 0)),
                  tab_spec, tab_spec, tab_spec],
        out_specs=pl.BlockSpec((BLOCK, att_w), lambda b, n: (qb0 + b * nqb + n, 0)),
        input_output_aliases={1: 0},
        compiler_params=_params(2),
        name="att_lat",
    )(sink, att_ctx, proj, kv, kv, kv, kv, kv, kv,
      cache_att_k.reshape(nbl, past, kv_w), cache_att_v.reshape(nbl, past, kv_w),
      cos_a, sin_hi_a, sin_lo_a)

    gr_blk = kv_off // tn
    mix_pre = pl.pallas_call(
        _merge_body,
        out_shape=jax.ShapeDtypeStruct((t, d), BF16),
        grid=(d // tn, t // tm),
        in_specs=[pl.BlockSpec((tm, ret_w), lambda j, i: (i, 0)),
                  pl.BlockSpec((tm, att_w), lambda j, i: (i, 0)),
                  pl.BlockSpec((ret_w, tn), lambda j, i: (0, j)),
                  pl.BlockSpec((att_w, tn), lambda j, i: (0, j)),
                  pl.BlockSpec((tm, tn), lambda j, i: (i, gr_blk + j)),
                  pl.BlockSpec((tm, tn), lambda j, i: (i, gr_blk + d // tn + j))],
        out_specs=pl.BlockSpec((tm, tn), lambda j, i: (i, j)),
        compiler_params=_params(2),
        name="merge",
    )(ret_out, att_out, w_branch_ret.reshape(ret_w, d), w_branch_att.reshape(att_w, d), proj, proj)

    mix = _matmul(mix_pre, w_out.reshape(d, d), n_out=d, w_col_block=lambda j: j,
                  out_dtype=F32, tm=tm, tn=tn, name="out_proj")

    tr = 128
    g = group_of(tr)
    x1, h2 = pl.pallas_call(
        functools.partial(_post1_body, tc // tr),
        out_shape=(jax.ShapeDtypeStruct((t, d), F32), jax.ShapeDtypeStruct((t, d), BF16)),
        grid=(t // tr,),
        in_specs=x_specs(tr) + [row_spec(tr, d), _mods_spec(d, 2, g), vec_spec, vec_spec,
                                _mods_spec(d, 3, g), _mods_spec(d, 4, g)],
        out_specs=(row_spec(tr, d), row_spec(tr, d)),
        compiler_params=_params(1),
        name="post_mix",
    )(xp, xs, mix, mods, vec(norm_mix_post), vec(norm_ffn_pre), mods, mods)

    tnh = 256
    w_up = w_ffn_up.reshape(d, 2 * hidden)
    act = pl.pallas_call(
        _ffn_up_body,
        out_shape=jax.ShapeDtypeStruct((t, hidden), BF16),
        grid=(hidden // tnh, t // tm),
        in_specs=[pl.BlockSpec((tm, d), lambda j, i: (i, 0)),
                  pl.BlockSpec((d, tnh), lambda j, i: (0, j)),
                  pl.BlockSpec((d, tnh), lambda j, i: (0, hidden // tnh + j))],
        out_specs=pl.BlockSpec((tm, tnh), lambda j, i: (i, j)),
        compiler_params=_params(2),
        name="ffn_up",
    )(h2, w_up, w_up)

    w_down = w_ffn_down.reshape(hidden, d)
    tw = 256
    w_down16 = pl.pallas_call(
        _cast_body,
        out_shape=jax.ShapeDtypeStruct((hidden, d), BF16),
        grid=(hidden // tw,),
        in_specs=[pl.BlockSpec((tw, d), lambda i: (i, 0))],
        out_specs=pl.BlockSpec((tw, d), lambda i: (i, 0)),
        compiler_params=_params(1),
        name="cast_w_down",
    )(w_down)
    f = _matmul(act, w_down16, n_out=d, w_col_block=lambda j: j, out_dtype=F32, tm=512, tn=tn,
                name="ffn_down")

    def final(rows, off_tiles, name):
        gg = group_of(tr)
        return pl.pallas_call(
            _final_body,
            out_shape=jax.ShapeDtypeStruct((rows, d), F32),
            grid=(rows // tr,),
            in_specs=[row_spec(tr, d, off_tiles), row_spec(tr, d, off_tiles),
                      pl.BlockSpec((None, None, 1, d), lambda i: (gg(i + off_tiles), 5, 0, 0)), vec_spec],
            out_specs=row_spec(tr, d),
            compiler_params=_params(1),
            name=name,
        )(x1, f, mods, vec(norm_ffn_post))

    y_prompt = final(tc, 0, "final_ctx").reshape(nbc, lc, d)
    y_sample = final(tl, tc // tr, "final_lat").reshape(nbl, ll, d)

    new_k = kv[:tc, :kv_w].reshape(nbc, 1, lc, ATT_KV_HEADS, hd)
    new_v = kv[:tc, kv_w:].reshape(nbc, 1, lc, ATT_KV_HEADS, hd)
    return y_prompt, y_sample, new_k, new_v, s_f, s_b
```

```python
import functools

import jax
import jax.numpy as jnp
from jax import lax
from jax.experimental import pallas as pl
from jax.experimental.pallas import tpu as pltpu

F32 = jnp.float32
BF16 = jnp.bfloat16

GRID_W = 64
RET_HEADS = 8
ATT_HEADS = 16
ATT_KV_HEADS = 4
ATT_GROUPS = ATT_HEADS // ATT_KV_HEADS
WINDOW = 128
BLOCK = 128
ROPE_BASE = 10000.0
NORM_EPS = 1e-6
NEG_INF = -1e30

VMEM_LIMIT_BYTES = 56 * 1024 * 1024
LANES = 128
FFN_ROW_CHUNK = 512


def _params(n_axes):
    return pltpu.CompilerParams(
        dimension_semantics=("arbitrary",) * n_axes, vmem_limit_bytes=VMEM_LIMIT_BYTES)


def _row_tile(rows, candidates):
    return next(c for c in candidates if rows % c == 0)


def _rms_scale(x):
    return lax.rsqrt(jnp.mean(x * x, axis=-1, keepdims=True) + NORM_EPS)


def _dot(a, b):
    return jnp.dot(a, b, preferred_element_type=F32)


def _dot_nt(a, b):
    return lax.dot_general(a, b, (((1,), (1,)), ((), ())), preferred_element_type=F32)


def _adaln_body(c_ref, w_ref, b_ref, o_ref):
    o_ref[...] = _dot(jax.nn.silu(c_ref[...]), w_ref[...]) + b_ref[...]


def _adaln(cond, w_mod, b_mod, tn=512):
    rows, d = cond.shape
    n = w_mod.shape[1]
    return pl.pallas_call(
        _adaln_body,
        out_shape=jax.ShapeDtypeStruct((rows, n), F32),
        grid=(n // tn,),
        in_specs=[pl.BlockSpec((rows, d), lambda j: (0, 0)),
                  pl.BlockSpec((d, tn), lambda j: (0, j)),
                  pl.BlockSpec((1, tn), lambda j: (0, j))],
        out_specs=pl.BlockSpec((rows, tn), lambda j: (0, j)),
        compiler_params=_params(1),
        name="adaln",
    )(cond, w_mod, b_mod)


def _mods_spec(d, m, group_of):
    return pl.BlockSpec((None, None, 1, d), lambda i: (group_of(i), m, 0, 0))


def _prenorm_body(nc_tiles, xp_ref, xs_ref, w_ref, sh_ref, sc_ref, h_ref):
    def run(x_ref):
        x = x_ref[...]
        y = x * _rms_scale(x) * w_ref[...]
        h_ref[...] = (y * (1.0 + sc_ref[...]) + sh_ref[...]).astype(h_ref.dtype)

    i = pl.program_id(0)
    pl.when(i < nc_tiles)(lambda: run(xp_ref))
    pl.when(i >= nc_tiles)(lambda: run(xs_ref))


def _residual1(x_ref, mix_ref, g1_ref, wpost_ref):
    m = mix_ref[...].astype(F32)
    return x_ref[...] + g1_ref[...] * (m * _rms_scale(m) * wpost_ref[...])


def _post1_body(nc_tiles, xp_ref, xs_ref, mix_ref, g1_ref, wpost_ref, wpre_ref, sh_ref, sc_ref, h2_ref):
    def run(x_ref):
        x1 = _residual1(x_ref, mix_ref, g1_ref, wpost_ref)
        y = x1 * _rms_scale(x1) * wpre_ref[...]
        h2_ref[...] = (y * (1.0 + sc_ref[...]) + sh_ref[...]).astype(h2_ref.dtype)

    i = pl.program_id(0)
    pl.when(i < nc_tiles)(lambda: run(xp_ref))
    pl.when(i >= nc_tiles)(lambda: run(xs_ref))


def _final_body(x_ref, mix_ref, f_ref, g1_ref, wpost_ref, g2_ref, w_ref, y_ref):
    x1 = _residual1(x_ref, mix_ref, g1_ref, wpost_ref)
    f = f_ref[...].astype(F32)
    y_ref[...] = x1 + g2_ref[...] * (f * _rms_scale(f) * w_ref[...])


def _mm_body(x_ref, w_ref, o_ref):
    o_ref[...] = _dot(x_ref[...], w_ref[...].astype(BF16)).astype(o_ref.dtype)


def _matmul(x, w, *, n_out, w_col_block, out_dtype, tm, tn, name):
    m, k = x.shape
    return pl.pallas_call(
        _mm_body,
        out_shape=jax.ShapeDtypeStruct((m, n_out), out_dtype),
        grid=(n_out // tn, m // tm),
        in_specs=[pl.BlockSpec((tm, k), lambda j, i: (i, 0)),
                  pl.BlockSpec((k, tn), lambda j, i: (0, w_col_block(j)))],
        out_specs=pl.BlockSpec((tm, tn), lambda j, i: (i, j)),
        compiler_params=_params(2),
        name=name,
    )(x, w)


def _merge_body(r_ref, a_ref, wr_ref, wa_ref, gr_ref, ga_ref, o_ref):
    r = _dot(r_ref[...], wr_ref[...].astype(BF16))
    a = _dot(a_ref[...], wa_ref[...].astype(BF16))
    gr = jax.nn.sigmoid(gr_ref[...].astype(F32))
    ga = jax.nn.sigmoid(ga_ref[...].astype(F32))
    o_ref[...] = (gr * r + ga * a).astype(o_ref.dtype)


def _ffn_up_body(h_ref, wa_ref, wb_ref, wd_ref, o_ref, wd16_ref):
    wa = wa_ref[...].astype(BF16)
    wb = wb_ref[...].astype(BF16)
    for c in range(h_ref.shape[0] // FFN_ROW_CHUNK):
        rs = slice(c * FFN_ROW_CHUNK, (c + 1) * FFN_ROW_CHUNK)
        h = h_ref[rs, :]
        o_ref[rs, :] = (jax.nn.silu(_dot(h, wa)) * _dot(h, wb)).astype(o_ref.dtype)
    wd16_ref[...] = wd_ref[...].astype(wd16_ref.dtype)


def _decay_body(lc, ldf_ref, ldb_ref, d_ref, qf_ref, qb_ref, kf_ref, kb_ref):
    h = pl.program_id(0)
    r = pl.program_id(1)
    rb, l = d_ref.shape

    def log_gamma(ld_ref, shape):
        return -jnp.exp(jnp.full(shape, ld_ref[h], F32))

    def rows(shape):
        return (lax.broadcasted_iota(jnp.int32, shape, 0) + r * rb).astype(F32)

    def cols(shape):
        return lax.broadcasted_iota(jnp.int32, shape, 1).astype(F32)

    diff = rows((rb, l)) - cols((rb, l))
    d_f = jnp.where(diff >= 0, jnp.exp(log_gamma(ldf_ref, (rb, l)) * jnp.maximum(diff, 0.0)), 0.0)
    d_b = jnp.where(diff <= 0, jnp.exp(log_gamma(ldb_ref, (rb, l)) * jnp.maximum(-diff, 0.0)), 0.0)
    d_ref[...] = d_f + d_b
    qs = qf_ref.shape
    qf_ref[...] = jnp.exp(log_gamma(ldf_ref, qs) * (rows(qs) + 1.0))
    qb_ref[...] = jnp.exp(log_gamma(ldb_ref, qs) * (float(l) - rows(qs)))
    ks = kf_ref.shape
    kf_ref[...] = jnp.exp(log_gamma(ldf_ref, ks) * (float(lc - 1) - cols(ks)))
    kb_ref[...] = jnp.exp(log_gamma(ldb_ref, ks) * cols(ks))


def _decay_tables(ldf, ldb, l, lc, dk):
    rb = 256
    return pl.pallas_call(
        functools.partial(_decay_body, lc),
        out_shape=(jax.ShapeDtypeStruct((RET_HEADS, l, l), F32),
                   jax.ShapeDtypeStruct((RET_HEADS, l, dk), F32),
                   jax.ShapeDtypeStruct((RET_HEADS, l, dk), F32),
                   jax.ShapeDtypeStruct((RET_HEADS, 8, lc), F32),
                   jax.ShapeDtypeStruct((RET_HEADS, 8, lc), F32)),
        grid=(RET_HEADS, l // rb),
        in_specs=[pl.BlockSpec(memory_space=pltpu.SMEM), pl.BlockSpec(memory_space=pltpu.SMEM)],
        out_specs=(pl.BlockSpec((None, rb, l), lambda h, r: (h, r, 0)),
                   pl.BlockSpec((None, rb, dk), lambda h, r: (h, r, 0)),
                   pl.BlockSpec((None, rb, dk), lambda h, r: (h, r, 0)),
                   pl.BlockSpec((None, 8, lc), lambda h, r: (h, 0, 0)),
                   pl.BlockSpec((None, 8, lc), lambda h, r: (h, 0, 0))),
        compiler_params=_params(2),
        name="decay_tables",
    )(ldf, ldb)


def _gated_group_norm(o, g_bf16, gn):
    on = o * _rms_scale(o) * gn
    return jax.nn.silu(g_bf16.astype(F32)) * on


def _ret_ctx_body(dk, q_ref, k_ref, v_ref, g_ref, d_ref, kf_ref, kb_ref, gn_ref, o_ref, sf_ref, sb_ref):
    for h in range(RET_HEADS):
        sl = slice(h * dk, (h + 1) * dk)
        q = q_ref[:, sl]
        k32 = k_ref[:, sl].astype(F32) * (dk ** -0.5)
        v = v_ref[:, sl]
        p = (_dot_nt(q, k32.astype(BF16)) * d_ref[h]).astype(BF16)
        o = _dot(p, v)
        kt = k32.T
        sf_ref[h] = _dot((kt * kf_ref[h][0:1, :]).astype(BF16), v)
        sb_ref[h] = _dot((kt * kb_ref[h][0:1, :]).astype(BF16), v)
        o_ref[:, sl] = _gated_group_norm(o, g_ref[:, sl], gn_ref[:, sl]).astype(o_ref.dtype)


def _ret_lat_body(dk, rb, buf_ref, q_ref, k_ref, v_ref, g_ref, d_ref, qf_ref, qb_ref, cos_ref, sin_ref,
                  s0f_ref, s0b_ref, gn_ref, o_ref):
    del buf_ref
    cos = cos_ref[...]
    sin = sin_ref[...]

    def rope(x):
        swapped = jnp.concatenate(
            [pltpu.roll(x[:, c * LANES:(c + 1) * LANES], LANES // 2, 1) for c in range(dk // LANES)], axis=1)
        return x * cos + swapped * sin

    q = rope(q_ref[...].astype(F32)).astype(BF16)
    k = rope(k_ref[...].astype(F32) * (dk ** -0.5)).astype(BF16)
    v = v_ref[...]
    s0f = s0f_ref[...].astype(BF16)
    s0b = s0b_ref[...].astype(BF16)
    gn = gn_ref[...]
    for r in range(q.shape[0] // rb):
        rs = slice(r * rb, (r + 1) * rb)
        qr = q[rs]
        p = (_dot_nt(qr, k) * d_ref[rs, :]).astype(BF16)
        o = _dot(p, v) + qf_ref[rs, :] * _dot(qr, s0f) + qb_ref[rs, :] * _dot(qr, s0b)
        o_ref[rs, :] = _gated_group_norm(o, g_ref[rs, :], gn).astype(o_ref.dtype)


def _softmax_t(scores, sink_row):
    m = sink_row
    for s in scores:
        m = jnp.maximum(m, jnp.max(s, axis=0, keepdims=True))
    ps = [jnp.exp(s - m) for s in scores]
    denom = jnp.exp(sink_row - m)
    for p in ps:
        denom = denom + jnp.sum(p, axis=0, keepdims=True)
    return [p.astype(BF16) for p in ps], denom


def _sink_row(sink_ref, kh, queries):
    return jnp.concatenate(
        [jnp.full((1, queries), sink_ref[kh * ATT_GROUPS + g], F32) for g in range(ATT_GROUPS)], axis=1)


def _att_ctx_body(hd, sink_ref, q_ref, k_ref, v_ref, o_ref):
    rows = q_ref.shape[0]
    scale = hd ** -0.5
    for kh in range(ATT_KV_HEADS):
        cs = slice(kh * hd, (kh + 1) * hd)
        heads = [kh * ATT_GROUPS + g for g in range(ATT_GROUPS)]
        qg = jnp.concatenate([q_ref[:, hh * hd:(hh + 1) * hd] for hh in heads], axis=0)
        k = (k_ref[:, cs] * scale).astype(BF16)
        (p,), denom = _softmax_t([_dot_nt(k, qg)], _sink_row(sink_ref, kh, rows))
        o_t = _dot(v_ref[:, cs].T.astype(BF16), p) / denom
        for g, hh in enumerate(heads):
            o_ref[:, hh * hd:(hh + 1) * hd] = o_t[:, g * rows:(g + 1) * rows].T.astype(o_ref.dtype)


def _att_lat_body(hd, length, sink_ref, buf_ref, q_ref, kp_ref, kc_ref, kn_ref, vp_ref, vc_ref, vn_ref,
                  ck_ref, cv_ref, cos_ref, sa_ref, sb_ref, o_ref):
    del buf_ref
    n = pl.program_id(1)
    nb = length // BLOCK
    scale = hd ** -0.5

    def rope(x, blk):
        rows = pl.ds(pl.multiple_of(blk * BLOCK, BLOCK), BLOCK)
        return (x * cos_ref[rows, :] + pltpu.roll(x, 3 * hd // 4, 1) * sa_ref[rows, :]
                + pltpu.roll(x, hd // 4, 1) * sb_ref[rows, :])

    bp = jnp.maximum(n - 1, 0)
    bn = jnp.minimum(n + 1, nb - 1)
    kpos = (n - 1) * BLOCK + lax.broadcasted_iota(jnp.int32, (3 * BLOCK, BLOCK), 0)
    qpos = n * BLOCK + lax.broadcasted_iota(jnp.int32, (3 * BLOCK, BLOCK), 1)
    mask = (jnp.abs(kpos - qpos) <= WINDOW) & (kpos >= 0) & (kpos < length)
    mask = jnp.concatenate([mask] * ATT_GROUPS, axis=1)
    for kh in range(ATT_KV_HEADS):
        cs = slice(kh * hd, (kh + 1) * hd)
        heads = [kh * ATT_GROUPS + g for g in range(ATT_GROUPS)]
        kw = (jnp.concatenate(
            [rope(kp_ref[:, cs], bp), rope(kc_ref[:, cs], n), rope(kn_ref[:, cs], bn)], axis=0)
              * scale).astype(BF16)
        vw_t = jnp.concatenate([vp_ref[:, cs], vc_ref[:, cs], vn_ref[:, cs]], axis=0).T.astype(BF16)
        qg = jnp.concatenate(
            [rope(q_ref[:, hh * hd:(hh + 1) * hd].astype(F32), n) for hh in heads], axis=0).astype(BF16)
        sw = jnp.where(mask, _dot_nt(kw, qg), NEG_INF)
        sc = _dot_nt((ck_ref[:, cs] * scale).astype(BF16), qg)
        (pw, pc), denom = _softmax_t([sw, sc], _sink_row(sink_ref, kh, BLOCK))
        o_t = (_dot(vw_t, pw) + _dot(cv_ref[:, cs].T.astype(BF16), pc)) / denom
        for g, hh in enumerate(heads):
            o_ref[:, hh * hd:(hh + 1) * hd] = o_t[:, g * BLOCK:(g + 1) * BLOCK].T.astype(o_ref.dtype)


def _rope_tables(length, hd):
    half = hd // 2
    quarter = half // 2
    rows = length // GRID_W
    row = jnp.repeat(jnp.arange(rows), GRID_W).astype(F32)
    col = jnp.tile(jnp.arange(GRID_W), rows).astype(F32)
    inv_freq = ROPE_BASE ** (-jnp.arange(quarter, dtype=F32) / quarter)
    ang_r = row[:, None] * inv_freq[None, :]
    ang_c = col[:, None] * inv_freq[None, :]
    zeros = jnp.zeros_like(ang_r)
    cos = jnp.concatenate([jnp.cos(ang_r)] * 2 + [jnp.cos(ang_c)] * 2, axis=-1)
    sin_hi = jnp.concatenate([-jnp.sin(ang_r), zeros, -jnp.sin(ang_c), zeros], axis=-1)
    sin_lo = jnp.concatenate([zeros, jnp.sin(ang_r), zeros, jnp.sin(ang_c)], axis=-1)
    return cos, sin_hi, sin_lo


def kernel(x_prompt, x_sample, cache_att_k, cache_att_v, state_ret_fwd, state_ret_bwd, c, c_ctx, w_mod, b_mod, norm_mix_pre, norm_mix_post, norm_ffn_pre, norm_ffn_post, w_in, ret_log_decay_fwd, ret_log_decay_bwd, ret_gn_w, att_sink, w_branch_ret, w_branch_att, w_out, w_ffn_up, w_ffn_down):
    nbc, lc, d = x_prompt.shape
    nbl, ll, _ = x_sample.shape
    depth = w_mod.shape[0]
    assert depth == 1
    tc = nbc * lc
    tl = nbl * ll
    t = tc + tl
    ret_w = w_branch_ret.shape[1]
    att_w = w_branch_att.shape[1]
    dk = ret_w // RET_HEADS
    hd = att_w // ATT_HEADS
    kv_w = ATT_KV_HEADS * hd
    hidden = w_ffn_down.shape[1]
    past = cache_att_k.shape[2]
    assert ret_w == att_w and ll % GRID_W == 0

    xp = x_prompt.reshape(tc, d)
    xs = x_sample.reshape(tl, d)
    w_in2 = w_in.reshape(d, -1)

    groups = 8
    assert 1 + nbl <= groups
    cond = jnp.concatenate([c_ctx[None, :], c, jnp.zeros((groups - 1 - nbl, d), F32)], axis=0)
    mods = _adaln(cond, w_mod.reshape(d, 6 * d), b_mod.reshape(1, 6 * d)).reshape(groups, 6, 1, d)

    def group_of(tr):
        return lambda i: jnp.where(i * tr < tc, 0, 1 + (i * tr - tc) // ll)

    def vec(a):
        return a.reshape(1, d)

    def row_spec(tr, width, off=0):
        return pl.BlockSpec((tr, width), lambda i: (i + off, 0))

    def x_specs(tr):
        nct = tc // tr
        return [pl.BlockSpec((tr, d), lambda i: (jnp.minimum(i, nct - 1), 0)),
                pl.BlockSpec((tr, d), lambda i: (jnp.maximum(i - nct, 0), 0))]

    vec_spec = pl.BlockSpec((1, d), lambda i: (0, 0))

    tr = 256
    g = group_of(tr)
    h = pl.pallas_call(
        functools.partial(_prenorm_body, tc // tr),
        out_shape=jax.ShapeDtypeStruct((t, d), BF16),
        grid=(t // tr,),
        in_specs=x_specs(tr) + [vec_spec, _mods_spec(d, 0, g), _mods_spec(d, 1, g)],
        out_specs=row_spec(tr, d),
        compiler_params=_params(1),
        name="prenorm",
    )(xp, xs, vec(norm_mix_pre), mods, mods)

    tm, tn = 1024, 512
    tm_big = _row_tile(t, (1536, 1024))
    kv_off = 4 * ret_w + att_w
    gate_off = kv_off + 2 * kv_w
    n_front = kv_off // tn
    proj = _matmul(h, w_in2, n_out=kv_off + 2 * d,
                   w_col_block=lambda j: jnp.where(j < n_front, j, j + (gate_off - kv_off) // tn),
                   out_dtype=BF16, tm=tm_big, tn=tn, name="in_proj")
    kv = _matmul(h, w_in2, n_out=2 * kv_w, w_col_block=lambda j: j + kv_off // tn,
                 out_dtype=F32, tm=tm_big, tn=tn, name="in_proj_kv")

    d_tab, qf_tab, qb_tab, kf_tab, kb_tab = _decay_tables(
        ret_log_decay_fwd.reshape(RET_HEADS), ret_log_decay_bwd.reshape(RET_HEADS), ll, lc, dk)
    gn = ret_gn_w.reshape(1, ret_w)
    ret_ctx, s_f, s_b = pl.pallas_call(
        functools.partial(_ret_ctx_body, dk),
        out_shape=(jax.ShapeDtypeStruct((t, ret_w), BF16),
                   jax.ShapeDtypeStruct((nbc, 1, RET_HEADS, dk, dk), F32),
                   jax.ShapeDtypeStruct((nbc, 1, RET_HEADS, dk, dk), F32)),
        grid=(nbc,),
        in_specs=[pl.BlockSpec((lc, ret_w), lambda b: (b, 0)),
                  pl.BlockSpec((lc, ret_w), lambda b: (b, 1)),
                  pl.BlockSpec((lc, ret_w), lambda b: (b, 2)),
                  pl.BlockSpec((lc, ret_w), lambda b: (b, 3)),
                  pl.BlockSpec((RET_HEADS, lc, lc), lambda b: (0, 0, 0)),
                  pl.BlockSpec((RET_HEADS, 8, lc), lambda b: (0, 0, 0)),
                  pl.BlockSpec((RET_HEADS, 8, lc), lambda b: (0, 0, 0)),
                  pl.BlockSpec((1, ret_w), lambda b: (0, 0))],
        out_specs=(pl.BlockSpec((lc, ret_w), lambda b: (b, 0)),
                   pl.BlockSpec((None, None, RET_HEADS, dk, dk), lambda b: (b, 0, 0, 0, 0)),
                   pl.BlockSpec((None, None, RET_HEADS, dk, dk), lambda b: (b, 0, 0, 0, 0))),
        compiler_params=_params(1),
        name="ret_ctx",
    )(proj, proj, proj, proj, d_tab, kf_tab, kb_tab, gn)

    cos_r, sin_hi_r, sin_lo_r = _rope_tables(ll, dk)
    rb0 = tc // ll
    hb = ret_w // dk
    ret_out = pl.pallas_call(
        functools.partial(_ret_lat_body, dk, 256),
        out_shape=jax.ShapeDtypeStruct((t, ret_w), BF16),
        grid=(RET_HEADS, nbl),
        in_specs=[pl.BlockSpec(memory_space=pl.ANY),
                  pl.BlockSpec((ll, dk), lambda hh, b: (rb0 + b, hh)),
                  pl.BlockSpec((ll, dk), lambda hh, b: (rb0 + b, hb + hh)),
                  pl.BlockSpec((ll, dk), lambda hh, b: (rb0 + b, 2 * hb + hh)),
                  pl.BlockSpec((ll, dk), lambda hh, b: (rb0 + b, 3 * hb + hh)),
                  pl.BlockSpec((None, ll, ll), lambda hh, b: (hh, 0, 0)),
                  pl.BlockSpec((None, ll, dk), lambda hh, b: (hh, 0, 0)),
                  pl.BlockSpec((None, ll, dk), lambda hh, b: (hh, 0, 0)),
                  pl.BlockSpec((ll, dk), lambda hh, b: (0, 0)),
                  pl.BlockSpec((ll, dk), lambda hh, b: (0, 0)),
                  pl.BlockSpec((None, None, None, dk, dk), lambda hh, b: (b, 0, hh, 0, 0)),
                  pl.BlockSpec((None, None, None, dk, dk), lambda hh, b: (b, 0, hh, 0, 0)),
                  pl.BlockSpec((1, dk), lambda hh, b: (0, hh))],
        out_specs=pl.BlockSpec((ll, dk), lambda hh, b: (rb0 + b, hh)),
        input_output_aliases={0: 0},
        compiler_params=_params(2),
        name="ret_lat",
    )(ret_ctx, proj, proj, proj, proj, d_tab, qf_tab, qb_tab, cos_r, sin_hi_r + sin_lo_r,
      state_ret_fwd, state_ret_bwd, gn)

    sink = att_sink.reshape(ATT_HEADS)
    aq_blk = 4 * ret_w // att_w
    att_ctx = pl.pallas_call(
        functools.partial(_att_ctx_body, hd),
        out_shape=jax.ShapeDtypeStruct((t, att_w), BF16),
        grid=(nbc,),
        in_specs=[pl.BlockSpec(memory_space=pltpu.SMEM),
                  pl.BlockSpec((lc, att_w), lambda b: (b, aq_blk)),
                  pl.BlockSpec((lc, kv_w), lambda b: (b, 0)),
                  pl.BlockSpec((lc, kv_w), lambda b: (b, 1))],
        out_specs=pl.BlockSpec((lc, att_w), lambda b: (b, 0)),
        compiler_params=_params(1),
        name="att_ctx",
    )(sink, proj, kv, kv)

    cos_a, sin_hi_a, sin_lo_a = _rope_tables(ll, hd)
    nqb = ll // BLOCK
    qb0 = tc // BLOCK

    def kv_spec(shift, col):
        return pl.BlockSpec(
            (BLOCK, kv_w), lambda b, n: (qb0 + b * nqb + jnp.clip(n + shift, 0, nqb - 1), col))

    tab_spec = pl.BlockSpec((ll, hd), lambda b, n: (0, 0))
    att_out = pl.pallas_call(
        functools.partial(_att_lat_body, hd, ll),
        out_shape=jax.ShapeDtypeStruct((t, att_w), BF16),
        grid=(nbl, nqb),
        in_specs=[pl.BlockSpec(memory_space=pltpu.SMEM),
                  pl.BlockSpec(memory_space=pl.ANY),
                  pl.BlockSpec((BLOCK, att_w), lambda b, n: (qb0 + b * nqb + n, aq_blk)),
                  kv_spec(-1, 0), kv_spec(0, 0), kv_spec(1, 0),
                  kv_spec(-1, 1), kv_spec(0, 1), kv_spec(1, 1),
                  pl.BlockSpec((None, past, kv_w), lambda b, n: (b, 0, 0)),
                  pl.BlockSpec((None, past, kv_w), lambda b, n: (b, 0, 0)),
                  tab_spec, tab_spec, tab_spec],
        out_specs=pl.BlockSpec((BLOCK, att_w), lambda b, n: (qb0 + b * nqb + n, 0)),
        input_output_aliases={1: 0},
        compiler_params=_params(2),
        name="att_lat",
    )(sink, att_ctx, proj, kv, kv, kv, kv, kv, kv,
      cache_att_k.reshape(nbl, past, kv_w), cache_att_v.reshape(nbl, past, kv_w),
      cos_a, sin_hi_a, sin_lo_a)

    gr_blk = kv_off // tn
    mix_pre = pl.pallas_call(
        _merge_body,
        out_shape=jax.ShapeDtypeStruct((t, d), BF16),
        grid=(d // tn, t // tm),
        in_specs=[pl.BlockSpec((tm, ret_w), lambda j, i: (i, 0)),
                  pl.BlockSpec((tm, att_w), lambda j, i: (i, 0)),
                  pl.BlockSpec((ret_w, tn), lambda j, i: (0, j)),
                  pl.BlockSpec((att_w, tn), lambda j, i: (0, j)),
                  pl.BlockSpec((tm, tn), lambda j, i: (i, gr_blk + j)),
                  pl.BlockSpec((tm, tn), lambda j, i: (i, gr_blk + d // tn + j))],
        out_specs=pl.BlockSpec((tm, tn), lambda j, i: (i, j)),
        compiler_params=_params(2),
        name="merge",
    )(ret_out, att_out, w_branch_ret.reshape(ret_w, d), w_branch_att.reshape(att_w, d), proj, proj)

    mix = _matmul(mix_pre, w_out.reshape(d, d), n_out=d, w_col_block=lambda j: j,
                  out_dtype=BF16, tm=tm_big, tn=tn, name="out_proj")

    tr = 256
    g = group_of(tr)
    h2 = pl.pallas_call(
        functools.partial(_post1_body, tc // tr),
        out_shape=jax.ShapeDtypeStruct((t, d), BF16),
        grid=(t // tr,),
        in_specs=x_specs(tr) + [row_spec(tr, d), _mods_spec(d, 2, g), vec_spec, vec_spec,
                                _mods_spec(d, 3, g), _mods_spec(d, 4, g)],
        out_specs=row_spec(tr, d),
        compiler_params=_params(1),
        name="post_mix",
    )(xp, xs, mix, mods, vec(norm_mix_post), vec(norm_ffn_pre), mods, mods)

    tnh = 256
    w_up = w_ffn_up.reshape(d, 2 * hidden)
    w_down = w_ffn_down.reshape(hidden, d)
    n_i = t // tm_big
    n_j = hidden // tnh
    wd_rows = hidden // (n_j * n_i)
    assert wd_rows * n_j * n_i == hidden and wd_rows % 16 == 0
    act, w_down16 = pl.pallas_call(
        _ffn_up_body,
        out_shape=(jax.ShapeDtypeStruct((t, hidden), BF16), jax.ShapeDtypeStruct((hidden, d), BF16)),
        grid=(n_j, n_i),
        in_specs=[pl.BlockSpec((tm_big, d), lambda j, i: (i, 0)),
                  pl.BlockSpec((d, tnh), lambda j, i: (0, j)),
                  pl.BlockSpec((d, tnh), lambda j, i: (0, n_j + j)),
                  pl.BlockSpec((wd_rows, d), lambda j, i: (j * n_i + i, 0))],
        out_specs=(pl.BlockSpec((tm_big, tnh), lambda j, i: (i, j)),
                   pl.BlockSpec((wd_rows, d), lambda j, i: (j * n_i + i, 0))),
        compiler_params=_params(2),
        name="ffn_up",
    )(h2, w_up, w_up, w_down)
    f = _matmul(act, w_down16, n_out=d, w_col_block=lambda j: j, out_dtype=BF16, tm=512, tn=tn,
                name="ffn_down")

    def final(x, off_tiles, name):
        rows = x.shape[0]
        gg = group_of(tr)

        def mod_spec(m):
            return pl.BlockSpec((None, None, 1, d), lambda i: (gg(i + off_tiles), m, 0, 0))

        return pl.pallas_call(
            _final_body,
            out_shape=jax.ShapeDtypeStruct((rows, d), F32),
            grid=(rows // tr,),
            in_specs=[row_spec(tr, d), row_spec(tr, d, off_tiles), row_spec(tr, d, off_tiles),
                      mod_spec(2), vec_spec, mod_spec(5), vec_spec],
            out_specs=row_spec(tr, d),
            compiler_params=_params(1),
            name=name,
        )(x, mix, f, mods, vec(norm_mix_post), mods, vec(norm_ffn_post))

    y_prompt = final(xp, 0, "final_ctx").reshape(nbc, lc, d)
    y_sample = final(xs, tc // tr, "final_lat").reshape(nbl, ll, d)

    new_k = kv[:tc, :kv_w].reshape(nbc, 1, lc, ATT_KV_HEADS, hd)
    new_v = kv[:tc, kv_w:].reshape(nbc, 1, lc, ATT_KV_HEADS, hd)
    return y_prompt, y_sample, new_k, new_v, s_f, s_b
```

```python
import functools

import jax
import jax.numpy as jnp
from jax import lax
from jax.experimental import pallas as pl
from jax.experimental.pallas import tpu as pltpu

F32 = jnp.float32
BF16 = jnp.bfloat16

GRID_W = 64
RET_HEADS = 8
ATT_HEADS = 16
ATT_KV_HEADS = 4
ATT_GROUPS = ATT_HEADS // ATT_KV_HEADS
WINDOW = 128
BLOCK = 128
ROPE_BASE = 10000.0
NORM_EPS = 1e-6
NEG_INF = -1e30

VMEM_LIMIT_BYTES = 56 * 1024 * 1024
LANES = 128
FFN_ROW_CHUNK = 512


def _params(n_axes):
    return pltpu.CompilerParams(
        dimension_semantics=("arbitrary",) * n_axes, vmem_limit_bytes=VMEM_LIMIT_BYTES)


def _row_tile(rows, candidates):
    return next(c for c in candidates if rows % c == 0)


def _rms_scale(x):
    return lax.rsqrt(jnp.mean(x * x, axis=-1, keepdims=True) + NORM_EPS)


def _dot(a, b):
    return jnp.dot(a, b, preferred_element_type=F32)


def _dot_nt(a, b):
    return lax.dot_general(a, b, (((1,), (1,)), ((), ())), preferred_element_type=F32)


def _adaln_body(c_ref, w_ref, b_ref, o_ref):
    o_ref[...] = _dot(jax.nn.silu(c_ref[...]), w_ref[...]) + b_ref[...]


def _adaln(cond, w_mod, b_mod, tn=512):
    rows, d = cond.shape
    n = w_mod.shape[1]
    return pl.pallas_call(
        _adaln_body,
        out_shape=jax.ShapeDtypeStruct((rows, n), F32),
        grid=(n // tn,),
        in_specs=[pl.BlockSpec((rows, d), lambda j: (0, 0)),
                  pl.BlockSpec((d, tn), lambda j: (0, j)),
                  pl.BlockSpec((1, tn), lambda j: (0, j))],
        out_specs=pl.BlockSpec((rows, tn), lambda j: (0, j)),
        compiler_params=_params(1),
        name="adaln",
    )(cond, w_mod, b_mod)


def _mods_spec(d, m, group_of):
    return pl.BlockSpec((None, None, 1, d), lambda i: (group_of(i), m, 0, 0))


def _prenorm_body(nc_tiles, xp_ref, xs_ref, w_ref, sh_ref, sc_ref, h_ref):
    def run(x_ref):
        x = x_ref[...]
        y = x * _rms_scale(x) * w_ref[...]
        h_ref[...] = (y * (1.0 + sc_ref[...]) + sh_ref[...]).astype(h_ref.dtype)

    i = pl.program_id(0)
    pl.when(i < nc_tiles)(lambda: run(xp_ref))
    pl.when(i >= nc_tiles)(lambda: run(xs_ref))


def _residual1(x_ref, mix_ref, g1_ref, wpost_ref):
    m = mix_ref[...].astype(F32)
    return x_ref[...] + g1_ref[...] * (m * _rms_scale(m) * wpost_ref[...])


def _post1_body(nc_tiles, xp_ref, xs_ref, mix_ref, g1_ref, wpost_ref, wpre_ref, sh_ref, sc_ref, h2_ref):
    def run(x_ref):
        x1 = _residual1(x_ref, mix_ref, g1_ref, wpost_ref)
        y = x1 * _rms_scale(x1) * wpre_ref[...]
        h2_ref[...] = (y * (1.0 + sc_ref[...]) + sh_ref[...]).astype(h2_ref.dtype)

    i = pl.program_id(0)
    pl.when(i < nc_tiles)(lambda: run(xp_ref))
    pl.when(i >= nc_tiles)(lambda: run(xs_ref))


def _final_body(x_ref, mix_ref, f_ref, g1_ref, wpost_ref, g2_ref, w_ref, y_ref):
    x1 = _residual1(x_ref, mix_ref, g1_ref, wpost_ref)
    f = f_ref[...].astype(F32)
    y_ref[...] = x1 + g2_ref[...] * (f * _rms_scale(f) * w_ref[...])


def _mm_body(x_ref, w_ref, o_ref):
    o_ref[...] = _dot(x_ref[...], w_ref[...].astype(BF16)).astype(o_ref.dtype)


def _matmul(x, w, *, n_out, w_col_block, out_dtype, tm, tn, name):
    m, k = x.shape
    return pl.pallas_call(
        _mm_body,
        out_shape=jax.ShapeDtypeStruct((m, n_out), out_dtype),
        grid=(n_out // tn, m // tm),
        in_specs=[pl.BlockSpec((tm, k), lambda j, i: (i, 0)),
                  pl.BlockSpec((k, tn), lambda j, i: (0, w_col_block(j)))],
        out_specs=pl.BlockSpec((tm, tn), lambda j, i: (i, j)),
        compiler_params=_params(2),
        name=name,
    )(x, w)


def _mm_stream_body(n_tiles, tm, x_hbm, w_ref, o_hbm, xbuf, obuf, in_sem, out_sem):
    j = pl.program_id(0)
    nj = pl.num_programs(0)
    tn = w_ref.shape[1]

    def x_copy(i, slot):
        return pltpu.make_async_copy(x_hbm.at[pl.ds(i * tm, tm), :], xbuf.at[slot], in_sem.at[slot])

    def o_copy(i, slot):
        return pltpu.make_async_copy(
            obuf.at[slot], o_hbm.at[pl.ds(i * tm, tm), pl.ds(j * tn, tn)], out_sem.at[slot])

    @pl.when(j == 0)
    def _():
        x_copy(0, 0).start()

    def pair(p, carry):
        for slot in (0, 1):
            i = 2 * p + slot
            x_copy(i, slot).wait()
            if slot == 0:
                x_copy(i + 1, 1).start()
            else:
                @pl.when(i + 1 < n_tiles)
                def _():
                    x_copy(i + 1, 0).start()

                @pl.when((i + 1 == n_tiles) & (j + 1 < nj))
                def _():
                    x_copy(0, 0).start()
            acc = _dot(xbuf[slot], w_ref[...].astype(BF16))

            @pl.when((p > 0) | (j > 0))
            def _():
                o_copy(i, slot).wait()
            obuf[slot] = acc.astype(obuf.dtype)
            o_copy(i, slot).start()
        return carry

    lax.fori_loop(0, n_tiles // 2, pair, 0)

    @pl.when(j == nj - 1)
    def _():
        for slot in (0, 1):
            o_copy(n_tiles - 2 + slot, slot).wait()


def _matmul_stream(x, w, *, n_out, w_col_block, out_dtype, tm, tn, name):
    m, k = x.shape
    n_tiles = m // tm
    assert n_tiles * tm == m and n_tiles % 2 == 0
    return pl.pallas_call(
        functools.partial(_mm_stream_body, n_tiles, tm),
        out_shape=jax.ShapeDtypeStruct((m, n_out), out_dtype),
        grid=(n_out // tn,),
        in_specs=[pl.BlockSpec(memory_space=pl.ANY),
                  pl.BlockSpec((k, tn), lambda j: (0, w_col_block(j)))],
        out_specs=pl.BlockSpec(memory_space=pl.ANY),
        scratch_shapes=[pltpu.VMEM((2, tm, k), x.dtype), pltpu.VMEM((2, tm, tn), out_dtype),
                        pltpu.SemaphoreType.DMA((2,)), pltpu.SemaphoreType.DMA((2,))],
        compiler_params=_params(1),
        name=name,
    )(x, w)


def _merge_body(r_ref, a_ref, wr_ref, wa_ref, gr_ref, ga_ref, o_ref):
    r = _dot(r_ref[...], wr_ref[...].astype(BF16))
    a = _dot(a_ref[...], wa_ref[...].astype(BF16))
    gr = jax.nn.sigmoid(gr_ref[...].astype(F32))
    ga = jax.nn.sigmoid(ga_ref[...].astype(F32))
    o_ref[...] = (gr * r + ga * a).astype(o_ref.dtype)


def _ffn_up_body(h_ref, wa_ref, wb_ref, wd_ref, o_ref, wd16_ref):
    wa = wa_ref[...].astype(BF16)
    wb = wb_ref[...].astype(BF16)
    for c in range(h_ref.shape[0] // FFN_ROW_CHUNK):
        rs = slice(c * FFN_ROW_CHUNK, (c + 1) * FFN_ROW_CHUNK)
        h = h_ref[rs, :]
        o_ref[rs, :] = (jax.nn.silu(_dot(h, wa)) * _dot(h, wb)).astype(o_ref.dtype)
    wd16_ref[...] = wd_ref[...].astype(wd16_ref.dtype)


def _decay_body(lc, ldf_ref, ldb_ref, d_ref, qf_ref, qb_ref, kf_ref, kb_ref):
    h = pl.program_id(0)
    r = pl.program_id(1)
    rb, l = d_ref.shape

    def log_gamma(ld_ref, shape):
        return -jnp.exp(jnp.full(shape, ld_ref[h], F32))

    def rows(shape):
        return (lax.broadcasted_iota(jnp.int32, shape, 0) + r * rb).astype(F32)

    def cols(shape):
        return lax.broadcasted_iota(jnp.int32, shape, 1).astype(F32)

    diff = rows((rb, l)) - cols((rb, l))
    d_f = jnp.where(diff >= 0, jnp.exp(log_gamma(ldf_ref, (rb, l)) * jnp.maximum(diff, 0.0)), 0.0)
    d_b = jnp.where(diff <= 0, jnp.exp(log_gamma(ldb_ref, (rb, l)) * jnp.maximum(-diff, 0.0)), 0.0)
    d_ref[...] = d_f + d_b
    qs = qf_ref.shape
    qf_ref[...] = jnp.exp(log_gamma(ldf_ref, qs) * (rows(qs) + 1.0))
    qb_ref[...] = jnp.exp(log_gamma(ldb_ref, qs) * (float(l) - rows(qs)))
    ks = kf_ref.shape
    kf_ref[...] = jnp.exp(log_gamma(ldf_ref, ks) * (float(lc - 1) - cols(ks)))
    kb_ref[...] = jnp.exp(log_gamma(ldb_ref, ks) * cols(ks))


def _decay_tables(ldf, ldb, l, lc, dk):
    rb = 256
    return pl.pallas_call(
        functools.partial(_decay_body, lc),
        out_shape=(jax.ShapeDtypeStruct((RET_HEADS, l, l), F32),
                   jax.ShapeDtypeStruct((RET_HEADS, l, dk), F32),
                   jax.ShapeDtypeStruct((RET_HEADS, l, dk), F32),
                   jax.ShapeDtypeStruct((RET_HEADS, 8, lc), F32),
                   jax.ShapeDtypeStruct((RET_HEADS, 8, lc), F32)),
        grid=(RET_HEADS, l // rb),
        in_specs=[pl.BlockSpec(memory_space=pltpu.SMEM), pl.BlockSpec(memory_space=pltpu.SMEM)],
        out_specs=(pl.BlockSpec((None, rb, l), lambda h, r: (h, r, 0)),
                   pl.BlockSpec((None, rb, dk), lambda h, r: (h, r, 0)),
                   pl.BlockSpec((None, rb, dk), lambda h, r: (h, r, 0)),
                   pl.BlockSpec((None, 8, lc), lambda h, r: (h, 0, 0)),
                   pl.BlockSpec((None, 8, lc), lambda h, r: (h, 0, 0))),
        compiler_params=_params(2),
        name="decay_tables",
    )(ldf, ldb)


def _gated_group_norm(o, g_bf16, gn):
    on = o * _rms_scale(o) * gn
    return jax.nn.silu(g_bf16.astype(F32)) * on


def _ctx_or_fill(n_ctx, o_ref, compute):
    b = pl.program_id(0)
    pl.when(b < n_ctx)(compute)

    @pl.when(b >= n_ctx)
    def _():
        o_ref[...] = jnp.zeros(o_ref.shape, o_ref.dtype)


def _ret_ctx_body(dk, n_ctx, q_ref, k_ref, v_ref, g_ref, d_ref, kf_ref, kb_ref, gn_ref, o_ref, sf_ref, sb_ref):
    def compute():
        for h in range(RET_HEADS):
            sl = slice(h * dk, (h + 1) * dk)
            q = q_ref[:, sl]
            k32 = k_ref[:, sl].astype(F32) * (dk ** -0.5)
            v = v_ref[:, sl]
            p = (_dot_nt(q, k32.astype(BF16)) * d_ref[h]).astype(BF16)
            o = _dot(p, v)
            kt = k32.T
            sf_ref[h] = _dot((kt * kf_ref[h][0:1, :]).astype(BF16), v)
            sb_ref[h] = _dot((kt * kb_ref[h][0:1, :]).astype(BF16), v)
            o_ref[:, sl] = _gated_group_norm(o, g_ref[:, sl], gn_ref[:, sl]).astype(o_ref.dtype)

    _ctx_or_fill(n_ctx, o_ref, compute)


def _ret_lat_body(dk, rb, buf_ref, q_ref, k_ref, v_ref, g_ref, d_ref, qf_ref, qb_ref, cos_ref, sin_ref,
                  s0f_ref, s0b_ref, gn_ref, o_ref):
    del buf_ref
    cos = cos_ref[...]
    sin = sin_ref[...]

    def rope(x):
        swapped = jnp.concatenate(
            [pltpu.roll(x[:, c * LANES:(c + 1) * LANES], LANES // 2, 1) for c in range(dk // LANES)], axis=1)
        return x * cos + swapped * sin

    q = rope(q_ref[...].astype(F32)).astype(BF16)
    k = rope(k_ref[...].astype(F32) * (dk ** -0.5)).astype(BF16)
    v = v_ref[...]
    s0f = s0f_ref[...].astype(BF16)
    s0b = s0b_ref[...].astype(BF16)
    gn = gn_ref[...]
    for r in range(q.shape[0] // rb):
        rs = slice(r * rb, (r + 1) * rb)
        qr = q[rs]
        p = (_dot_nt(qr, k) * d_ref[rs, :]).astype(BF16)
        o = _dot(p, v) + qf_ref[rs, :] * _dot(qr, s0f) + qb_ref[rs, :] * _dot(qr, s0b)
        o_ref[rs, :] = _gated_group_norm(o, g_ref[rs, :], gn).astype(o_ref.dtype)


def _softmax_t(scores, sink_row):
    m = sink_row
    for s in scores:
        m = jnp.maximum(m, jnp.max(s, axis=0, keepdims=True))
    ps = [jnp.exp(s - m) for s in scores]
    denom = jnp.exp(sink_row - m)
    for p in ps:
        denom = denom + jnp.sum(p, axis=0, keepdims=True)
    return [p.astype(BF16) for p in ps], denom


def _sink_row(sink_ref, kh, queries):
    return jnp.concatenate(
        [jnp.full((1, queries), sink_ref[kh * ATT_GROUPS + g], F32) for g in range(ATT_GROUPS)], axis=1)


def _att_ctx_body(hd, n_ctx, sink_ref, q_ref, k_ref, v_ref, o_ref, nk_ref, nv_ref):
    rows = q_ref.shape[0]
    scale = hd ** -0.5

    def compute():
        for kh in range(ATT_KV_HEADS):
            cs = slice(kh * hd, (kh + 1) * hd)
            heads = [kh * ATT_GROUPS + g for g in range(ATT_GROUPS)]
            qg = jnp.concatenate([q_ref[:, hh * hd:(hh + 1) * hd] for hh in heads], axis=0)
            k32 = k_ref[:, cs]
            v32 = v_ref[:, cs]
            nk_ref[pl.ds(kh, rows, stride=ATT_KV_HEADS), :] = k32
            nv_ref[pl.ds(kh, rows, stride=ATT_KV_HEADS), :] = v32
            k = (k32 * scale).astype(BF16)
            (p,), denom = _softmax_t([_dot_nt(k, qg)], _sink_row(sink_ref, kh, rows))
            o_t = _dot(v32.T.astype(BF16), p) / denom
            for g, hh in enumerate(heads):
                o_ref[:, hh * hd:(hh + 1) * hd] = o_t[:, g * rows:(g + 1) * rows].T.astype(o_ref.dtype)

    _ctx_or_fill(n_ctx, o_ref, compute)


def _att_lat_body(hd, length, sink_ref, buf_ref, q_ref, kp_ref, kc_ref, kn_ref, vp_ref, vc_ref, vn_ref,
                  ck_ref, cv_ref, cos_ref, sa_ref, sb_ref, o_ref):
    del buf_ref
    n = pl.program_id(1)
    nb = length // BLOCK
    scale = hd ** -0.5

    def rope(x, blk):
        rows = pl.ds(pl.multiple_of(blk * BLOCK, BLOCK), BLOCK)
        return (x * cos_ref[rows, :] + pltpu.roll(x, 3 * hd // 4, 1) * sa_ref[rows, :]
                + pltpu.roll(x, hd // 4, 1) * sb_ref[rows, :])

    bp = jnp.maximum(n - 1, 0)
    bn = jnp.minimum(n + 1, nb - 1)
    kpos = (n - 1) * BLOCK + lax.broadcasted_iota(jnp.int32, (3 * BLOCK, BLOCK), 0)
    qpos = n * BLOCK + lax.broadcasted_iota(jnp.int32, (3 * BLOCK, BLOCK), 1)
    mask = (jnp.abs(kpos - qpos) <= WINDOW) & (kpos >= 0) & (kpos < length)
    mask = jnp.concatenate([mask] * ATT_GROUPS, axis=1)
    for kh in range(ATT_KV_HEADS):
        cs = slice(kh * hd, (kh + 1) * hd)
        heads = [kh * ATT_GROUPS + g for g in range(ATT_GROUPS)]
        kw = (jnp.concatenate(
            [rope(kp_ref[:, cs], bp), rope(kc_ref[:, cs], n), rope(kn_ref[:, cs], bn)], axis=0)
              * scale).astype(BF16)
        vw_t = jnp.concatenate([vp_ref[:, cs], vc_ref[:, cs], vn_ref[:, cs]], axis=0).T.astype(BF16)
        qg = jnp.concatenate(
            [rope(q_ref[:, hh * hd:(hh + 1) * hd].astype(F32), n) for hh in heads], axis=0).astype(BF16)
        past = ck_ref.shape[0] // ATT_KV_HEADS
        ck = ck_ref[pl.ds(kh, past, stride=ATT_KV_HEADS), :]
        cv = cv_ref[pl.ds(kh, past, stride=ATT_KV_HEADS), :]
        sw = jnp.where(mask, _dot_nt(kw, qg), NEG_INF)
        sc = _dot_nt((ck * scale).astype(BF16), qg)
        (pw, pc), denom = _softmax_t([sw, sc], _sink_row(sink_ref, kh, BLOCK))
        o_t = (_dot(vw_t, pw) + _dot(cv.T.astype(BF16), pc)) / denom
        for g, hh in enumerate(heads):
            o_ref[:, hh * hd:(hh + 1) * hd] = o_t[:, g * BLOCK:(g + 1) * BLOCK].T.astype(o_ref.dtype)


def _rope_tables(length, hd):
    half = hd // 2
    quarter = half // 2
    rows = length // GRID_W
    row = jnp.repeat(jnp.arange(rows), GRID_W).astype(F32)
    col = jnp.tile(jnp.arange(GRID_W), rows).astype(F32)
    inv_freq = ROPE_BASE ** (-jnp.arange(quarter, dtype=F32) / quarter)
    ang_r = row[:, None] * inv_freq[None, :]
    ang_c = col[:, None] * inv_freq[None, :]
    zeros = jnp.zeros_like(ang_r)
    cos = jnp.concatenate([jnp.cos(ang_r)] * 2 + [jnp.cos(ang_c)] * 2, axis=-1)
    sin_hi = jnp.concatenate([-jnp.sin(ang_r), zeros, -jnp.sin(ang_c), zeros], axis=-1)
    sin_lo = jnp.concatenate([zeros, jnp.sin(ang_r), zeros, jnp.sin(ang_c)], axis=-1)
    return cos, sin_hi, sin_lo


def kernel(x_prompt, x_sample, cache_att_k, cache_att_v, state_ret_fwd, state_ret_bwd, c, c_ctx, w_mod, b_mod, norm_mix_pre, norm_mix_post, norm_ffn_pre, norm_ffn_post, w_in, ret_log_decay_fwd, ret_log_decay_bwd, ret_gn_w, att_sink, w_branch_ret, w_branch_att, w_out, w_ffn_up, w_ffn_down):
    nbc, lc, d = x_prompt.shape
    nbl, ll, _ = x_sample.shape
    depth = w_mod.shape[0]
    assert depth == 1
    tc = nbc * lc
    tl = nbl * ll
    t = tc + tl
    ret_w = w_branch_ret.shape[1]
    att_w = w_branch_att.shape[1]
    dk = ret_w // RET_HEADS
    hd = att_w // ATT_HEADS
    kv_w = ATT_KV_HEADS * hd
    hidden = w_ffn_down.shape[1]
    past = cache_att_k.shape[2]
    assert ret_w == att_w and ll % GRID_W == 0

    xp = x_prompt.reshape(tc, d)
    xs = x_sample.reshape(tl, d)
    w_in2 = w_in.reshape(d, -1)

    groups = 8
    assert 1 + nbl <= groups
    cond = jnp.concatenate([c_ctx[None, :], c, jnp.zeros((groups - 1 - nbl, d), F32)], axis=0)
    mods = _adaln(cond, w_mod.reshape(d, 6 * d), b_mod.reshape(1, 6 * d)).reshape(groups, 6, 1, d)

    def group_of(tr):
        return lambda i: jnp.where(i * tr < tc, 0, 1 + (i * tr - tc) // ll)

    def vec(a):
        return a.reshape(1, d)

    def row_spec(tr, width, off=0):
        return pl.BlockSpec((tr, width), lambda i: (i + off, 0))

    def x_specs(tr):
        nct = tc // tr
        return [pl.BlockSpec((tr, d), lambda i: (jnp.minimum(i, nct - 1), 0)),
                pl.BlockSpec((tr, d), lambda i: (jnp.maximum(i - nct, 0), 0))]

    vec_spec = pl.BlockSpec((1, d), lambda i: (0, 0))

    tr = 256
    g = group_of(tr)
    h = pl.pallas_call(
        functools.partial(_prenorm_body, tc // tr),
        out_shape=jax.ShapeDtypeStruct((t, d), BF16),
        grid=(t // tr,),
        in_specs=x_specs(tr) + [vec_spec, _mods_spec(d, 0, g), _mods_spec(d, 1, g)],
        out_specs=row_spec(tr, d),
        compiler_params=_params(1),
        name="prenorm",
    )(xp, xs, vec(norm_mix_pre), mods, mods)

    tm, tn = 1024, 512
    tm_big = _row_tile(t, (1536, 1024))
    kv_off = 4 * ret_w + att_w
    gate_off = kv_off + 2 * kv_w
    n_front = kv_off // tn
    proj = _matmul_stream(h, w_in2, n_out=kv_off + 2 * d,
                          w_col_block=lambda j: jnp.where(j < n_front, j, j + (gate_off - kv_off) // tn),
                          out_dtype=BF16, tm=tm_big, tn=tn, name="in_proj")
    kv = _matmul(h, w_in2, n_out=2 * kv_w, w_col_block=lambda j: j + kv_off // tn,
                 out_dtype=F32, tm=tm_big, tn=tn, name="in_proj_kv")

    d_tab, qf_tab, qb_tab, kf_tab, kb_tab = _decay_tables(
        ret_log_decay_fwd.reshape(RET_HEADS), ret_log_decay_bwd.reshape(RET_HEADS), ll, lc, dk)
    gn = ret_gn_w.reshape(1, ret_w)
    ctx_steps = t // lc

    def cb(b):
        return jnp.minimum(b, nbc - 1)

    state_spec = pl.BlockSpec((None, None, RET_HEADS, dk, dk), lambda b: (cb(b), 0, 0, 0, 0))
    ret_ctx, s_f, s_b = pl.pallas_call(
        functools.partial(_ret_ctx_body, dk, nbc),
        out_shape=(jax.ShapeDtypeStruct((t, ret_w), BF16),
                   jax.ShapeDtypeStruct((nbc, 1, RET_HEADS, dk, dk), F32),
                   jax.ShapeDtypeStruct((nbc, 1, RET_HEADS, dk, dk), F32)),
        grid=(ctx_steps,),
        in_specs=[pl.BlockSpec((lc, ret_w), lambda b: (cb(b), 0)),
                  pl.BlockSpec((lc, ret_w), lambda b: (cb(b), 1)),
                  pl.BlockSpec((lc, ret_w), lambda b: (cb(b), 2)),
                  pl.BlockSpec((lc, ret_w), lambda b: (cb(b), 3)),
                  pl.BlockSpec((RET_HEADS, lc, lc), lambda b: (0, 0, 0)),
                  pl.BlockSpec((RET_HEADS, 8, lc), lambda b: (0, 0, 0)),
                  pl.BlockSpec((RET_HEADS, 8, lc), lambda b: (0, 0, 0)),
                  pl.BlockSpec((1, ret_w), lambda b: (0, 0))],
        out_specs=(pl.BlockSpec((lc, ret_w), lambda b: (b, 0)), state_spec, state_spec),
        compiler_params=_params(1),
        name="ret_ctx",
    )(proj, proj, proj, proj, d_tab, kf_tab, kb_tab, gn)

    cos_r, sin_hi_r, sin_lo_r = _rope_tables(ll, dk)
    rb0 = tc // ll
    hb = ret_w // dk
    ret_out = pl.pallas_call(
        functools.partial(_ret_lat_body, dk, 256),
        out_shape=jax.ShapeDtypeStruct((t, ret_w), BF16),
        grid=(RET_HEADS, nbl),
        in_specs=[pl.BlockSpec(memory_space=pl.ANY),
                  pl.BlockSpec((ll, dk), lambda hh, b: (rb0 + b, hh)),
                  pl.BlockSpec((ll, dk), lambda hh, b: (rb0 + b, hb + hh)),
                  pl.BlockSpec((ll, dk), lambda hh, b: (rb0 + b, 2 * hb + hh)),
                  pl.BlockSpec((ll, dk), lambda hh, b: (rb0 + b, 3 * hb + hh)),
                  pl.BlockSpec((None, ll, ll), lambda hh, b: (hh, 0, 0)),
                  pl.BlockSpec((None, ll, dk), lambda hh, b: (hh, 0, 0)),
                  pl.BlockSpec((None, ll, dk), lambda hh, b: (hh, 0, 0)),
                  pl.BlockSpec((ll, dk), lambda hh, b: (0, 0)),
                  pl.BlockSpec((ll, dk), lambda hh, b: (0, 0)),
                  pl.BlockSpec((None, None, None, dk, dk), lambda hh, b: (b, 0, hh, 0, 0)),
                  pl.BlockSpec((None, None, None, dk, dk), lambda hh, b: (b, 0, hh, 0, 0)),
                  pl.BlockSpec((1, dk), lambda hh, b: (0, hh))],
        out_specs=pl.BlockSpec((ll, dk), lambda hh, b: (rb0 + b, hh)),
        input_output_aliases={0: 0},
        compiler_params=_params(2),
        name="ret_lat",
    )(ret_ctx, proj, proj, proj, proj, d_tab, qf_tab, qb_tab, cos_r, sin_hi_r + sin_lo_r,
      state_ret_fwd, state_ret_bwd, gn)

    sink = att_sink.reshape(ATT_HEADS)
    aq_blk = 4 * ret_w // att_w
    new_kv_spec = pl.BlockSpec((lc * ATT_KV_HEADS, hd), lambda b: (cb(b), 0))
    att_ctx, new_k, new_v = pl.pallas_call(
        functools.partial(_att_ctx_body, hd, nbc),
        out_shape=(jax.ShapeDtypeStruct((t, att_w), BF16),
                   jax.ShapeDtypeStruct((tc * ATT_KV_HEADS, hd), F32),
                   jax.ShapeDtypeStruct((tc * ATT_KV_HEADS, hd), F32)),
        grid=(ctx_steps,),
        in_specs=[pl.BlockSpec(memory_space=pltpu.SMEM),
                  pl.BlockSpec((lc, att_w), lambda b: (cb(b), aq_blk)),
                  pl.BlockSpec((lc, kv_w), lambda b: (cb(b), 0)),
                  pl.BlockSpec((lc, kv_w), lambda b: (cb(b), 1))],
        out_specs=(pl.BlockSpec((lc, att_w), lambda b: (b, 0)), new_kv_spec, new_kv_spec),
        compiler_params=_params(1),
        name="att_ctx",
    )(sink, proj, kv, kv)
    new_k = new_k.reshape(nbc, 1, lc, ATT_KV_HEADS, hd)
    new_v = new_v.reshape(nbc, 1, lc, ATT_KV_HEADS, hd)

    cos_a, sin_hi_a, sin_lo_a = _rope_tables(ll, hd)
    nqb = ll // BLOCK
    qb0 = tc // BLOCK

    def kv_spec(shift, col):
        return pl.BlockSpec(
            (BLOCK, kv_w), lambda b, n: (qb0 + b * nqb + jnp.clip(n + shift, 0, nqb - 1), col))

    tab_spec = pl.BlockSpec((ll, hd), lambda b, n: (0, 0))
    att_out = pl.pallas_call(
        functools.partial(_att_lat_body, hd, ll),
        out_shape=jax.ShapeDtypeStruct((t, att_w), BF16),
        grid=(nbl, nqb),
        in_specs=[pl.BlockSpec(memory_space=pltpu.SMEM),
                  pl.BlockSpec(memory_space=pl.ANY),
                  pl.BlockSpec((BLOCK, att_w), lambda b, n: (qb0 + b * nqb + n, aq_blk)),
                  kv_spec(-1, 0), kv_spec(0, 0), kv_spec(1, 0),
                  kv_spec(-1, 1), kv_spec(0, 1), kv_spec(1, 1),
                  pl.BlockSpec((past * ATT_KV_HEADS, hd), lambda b, n: (b, 0)),
                  pl.BlockSpec((past * ATT_KV_HEADS, hd), lambda b, n: (b, 0)),
                  tab_spec, tab_spec, tab_spec],
        out_specs=pl.BlockSpec((BLOCK, att_w), lambda b, n: (qb0 + b * nqb + n, 0)),
        input_output_aliases={1: 0},
        compiler_params=_params(2),
        name="att_lat",
    )(sink, att_ctx, proj, kv, kv, kv, kv, kv, kv,
      cache_att_k.reshape(nbl * past * ATT_KV_HEADS, hd), cache_att_v.reshape(nbl * past * ATT_KV_HEADS, hd),
      cos_a, sin_hi_a, sin_lo_a)

    gr_blk = kv_off // tn
    mix_pre = pl.pallas_call(
        _merge_body,
        out_shape=jax.ShapeDtypeStruct((t, d), BF16),
        grid=(d // tn, t // tm),
        in_specs=[pl.BlockSpec((tm, ret_w), lambda j, i: (i, 0)),
                  pl.BlockSpec((tm, att_w), lambda j, i: (i, 0)),
                  pl.BlockSpec((ret_w, tn), lambda j, i: (0, j)),
                  pl.BlockSpec((att_w, tn), lambda j, i: (0, j)),
                  pl.BlockSpec((tm, tn), lambda j, i: (i, gr_blk + j)),
                  pl.BlockSpec((tm, tn), lambda j, i: (i, gr_blk + d // tn + j))],
        out_specs=pl.BlockSpec((tm, tn), lambda j, i: (i, j)),
        compiler_params=_params(2),
        name="merge",
    )(ret_out, att_out, w_branch_ret.reshape(ret_w, d), w_branch_att.reshape(att_w, d), proj, proj)

    mix = _matmul(mix_pre, w_out.reshape(d, d), n_out=d, w_col_block=lambda j: j,
                  out_dtype=BF16, tm=tm_big, tn=tn, name="out_proj")

    tr = 256
    g = group_of(tr)
    h2 = pl.pallas_call(
        functools.partial(_post1_body, tc // tr),
        out_shape=jax.ShapeDtypeStruct((t, d), BF16),
        grid=(t // tr,),
        in_specs=x_specs(tr) + [row_spec(tr, d), _mods_spec(d, 2, g), vec_spec, vec_spec,
                                _mods_spec(d, 3, g), _mods_spec(d, 4, g)],
        out_specs=row_spec(tr, d),
        compiler_params=_params(1),
        name="post_mix",
    )(xp, xs, mix, mods, vec(norm_mix_post), vec(norm_ffn_pre), mods, mods)

    tnh = 256
    w_up = w_ffn_up.reshape(d, 2 * hidden)
    w_down = w_ffn_down.reshape(hidden, d)
    n_i = t // tm_big
    n_j = hidden // tnh
    wd_rows = hidden // (n_j * n_i)
    assert wd_rows * n_j * n_i == hidden and wd_rows % 16 == 0
    act, w_down16 = pl.pallas_call(
        _ffn_up_body,
        out_shape=(jax.ShapeDtypeStruct((t, hidden), BF16), jax.ShapeDtypeStruct((hidden, d), BF16)),
        grid=(n_j, n_i),
        in_specs=[pl.BlockSpec((tm_big, d), lambda j, i: (i, 0)),
                  pl.BlockSpec((d, tnh), lambda j, i: (0, j)),
                  pl.BlockSpec((d, tnh), lambda j, i: (0, n_j + j)),
                  pl.BlockSpec((wd_rows, d), lambda j, i: (j * n_i + i, 0))],
        out_specs=(pl.BlockSpec((tm_big, tnh), lambda j, i: (i, j)),
                   pl.BlockSpec((wd_rows, d), lambda j, i: (j * n_i + i, 0))),
        compiler_params=_params(2),
        name="ffn_up",
    )(h2, w_up, w_up, w_down)
    f = _matmul(act, w_down16, n_out=d, w_col_block=lambda j: j, out_dtype=BF16, tm=512, tn=tn,
                name="ffn_down")

    def final(x, off_tiles, name):
        rows = x.shape[0]
        gg = group_of(tr)

        def mod_spec(m):
            return pl.BlockSpec((None, None, 1, d), lambda i: (gg(i + off_tiles), m, 0, 0))

        return pl.pallas_call(
            _final_body,
            out_shape=jax.ShapeDtypeStruct((rows, d), F32),
            grid=(rows // tr,),
            in_specs=[row_spec(tr, d), row_spec(tr, d, off_tiles), row_spec(tr, d, off_tiles),
                      mod_spec(2), vec_spec, mod_spec(5), vec_spec],
            out_specs=row_spec(tr, d),
            compiler_params=_params(1),
            name=name,
        )(x, mix, f, mods, vec(norm_mix_post), mods, vec(norm_ffn_post))

    y_prompt = final(xp, 0, "final_ctx").reshape(nbc, lc, d)
    y_sample = final(xs, tc // tr, "final_lat").reshape(nbl, ll, d)

    return y_prompt, y_sample, new_k, new_v, s_f, s_b
```

```python
import functools

import jax
import jax.numpy as jnp
from jax import lax
from jax.experimental import pallas as pl
from jax.experimental.pallas import tpu as pltpu

F32 = jnp.float32
BF16 = jnp.bfloat16

GRID_W = 64
RET_HEADS = 8
ATT_HEADS = 16
ATT_KV_HEADS = 4
ATT_GROUPS = ATT_HEADS // ATT_KV_HEADS
WINDOW = 128
BLOCK = 128
ROPE_BASE = 10000.0
NORM_EPS = 1e-6
NEG_INF = -1e30

VMEM_LIMIT_BYTES = 56 * 1024 * 1024
LANES = 128
FFN_ROW_CHUNK = 512
MERGE_ROW_CHUNK = 512


def _params(n_axes):
    return pltpu.CompilerParams(
        dimension_semantics=("arbitrary",) * n_axes, vmem_limit_bytes=VMEM_LIMIT_BYTES)


def _row_tile(rows, candidates):
    return next(c for c in candidates if rows % c == 0)


def _rms_scale(x):
    return lax.rsqrt(jnp.mean(x * x, axis=-1, keepdims=True) + NORM_EPS)


def _dot(a, b):
    return jnp.dot(a, b, preferred_element_type=F32)


def _dot_nt(a, b):
    return lax.dot_general(a, b, (((1,), (1,)), ((), ())), preferred_element_type=F32)


def _adaln_body(c_ref, w_ref, b_ref, o_ref):
    o_ref[...] = _dot(jax.nn.silu(c_ref[...]), w_ref[...]) + b_ref[...]


def _adaln(cond, w_mod, b_mod, tn=512):
    rows, d = cond.shape
    n = w_mod.shape[1]
    return pl.pallas_call(
        _adaln_body,
        out_shape=jax.ShapeDtypeStruct((rows, n), F32),
        grid=(n // tn,),
        in_specs=[pl.BlockSpec((rows, d), lambda j: (0, 0)),
                  pl.BlockSpec((d, tn), lambda j: (0, j)),
                  pl.BlockSpec((1, tn), lambda j: (0, j))],
        out_specs=pl.BlockSpec((rows, tn), lambda j: (0, j)),
        compiler_params=_params(1),
        name="adaln",
    )(cond, w_mod, b_mod)


def _mods_spec(d, m, group_of):
    return pl.BlockSpec((None, None, 1, d), lambda i: (group_of(i), m, 0, 0))


def _prenorm_body(nc_tiles, xp_ref, xs_ref, w_ref, sh_ref, sc_ref, h_ref):
    def run(x_ref):
        x = x_ref[...]
        y = x * _rms_scale(x) * w_ref[...]
        h_ref[...] = (y * (1.0 + sc_ref[...]) + sh_ref[...]).astype(h_ref.dtype)

    i = pl.program_id(0)
    pl.when(i < nc_tiles)(lambda: run(xp_ref))
    pl.when(i >= nc_tiles)(lambda: run(xs_ref))


def _residual1(x_ref, mix_ref, g1_ref, wpost_ref):
    m = mix_ref[...].astype(F32)
    return x_ref[...] + g1_ref[...] * (m * _rms_scale(m) * wpost_ref[...])


def _post1_body(nc_tiles, xp_ref, xs_ref, mix_ref, g1_ref, wpost_ref, wpre_ref, sh_ref, sc_ref, h2_ref):
    def run(x_ref):
        x1 = _residual1(x_ref, mix_ref, g1_ref, wpost_ref)
        y = x1 * _rms_scale(x1) * wpre_ref[...]
        h2_ref[...] = (y * (1.0 + sc_ref[...]) + sh_ref[...]).astype(h2_ref.dtype)

    i = pl.program_id(0)
    pl.when(i < nc_tiles)(lambda: run(xp_ref))
    pl.when(i >= nc_tiles)(lambda: run(xs_ref))


def _final_body(x_ref, mix_ref, f_ref, g1_ref, wpost_ref, g2_ref, w_ref, y_ref):
    x1 = _residual1(x_ref, mix_ref, g1_ref, wpost_ref)
    f = f_ref[...].astype(F32)
    y_ref[...] = x1 + g2_ref[...] * (f * _rms_scale(f) * w_ref[...])


def _mm_body(n_cast, x_ref, w_ref, *refs):
    o_ref = refs[n_cast]
    o_ref[...] = _dot(x_ref[...], w_ref[...].astype(BF16)).astype(o_ref.dtype)
    for src_ref, dst_ref in zip(refs[:n_cast], refs[n_cast + 1:]):
        dst_ref[...] = src_ref[...].astype(dst_ref.dtype)


def _cast_block_grid(shape, col_blocks, min_rows, steps):
    row_blocks = 1
    while (row_blocks * 2 * col_blocks <= steps and shape[0] % (row_blocks * 2) == 0
           and (shape[0] // (row_blocks * 2)) % min_rows == 0):
        row_blocks *= 2
    return row_blocks, col_blocks


def _matmul(x, w, *, n_out, w_col_block, out_dtype, tm, tn, name, cast_along=()):
    m, k = x.shape
    n_i = m // tm
    steps = (n_out // tn) * n_i
    cast_specs, cast_shapes = [], []
    for src, col_blocks in cast_along:
        nrb, ncb = _cast_block_grid(src.shape, col_blocks, 16, steps)
        n_blocks = nrb * ncb

        def index(j, i, ncb=ncb, n_blocks=n_blocks):
            blk = jnp.minimum(j * n_i + i, n_blocks - 1)
            return blk // ncb, blk % ncb

        cast_specs.append(pl.BlockSpec((src.shape[0] // nrb, src.shape[1] // ncb), index))
        cast_shapes.append(jax.ShapeDtypeStruct(src.shape, BF16))
    outs = pl.pallas_call(
        functools.partial(_mm_body, len(cast_along)),
        out_shape=[jax.ShapeDtypeStruct((m, n_out), out_dtype)] + cast_shapes,
        grid=(n_out // tn, n_i),
        in_specs=[pl.BlockSpec((tm, k), lambda j, i: (i, 0)),
                  pl.BlockSpec((k, tn), lambda j, i: (0, w_col_block(j)))] + cast_specs,
        out_specs=[pl.BlockSpec((tm, tn), lambda j, i: (i, j))] + cast_specs,
        compiler_params=_params(2),
        name=name,
    )(x, w, *[src for src, _ in cast_along])
    return outs if cast_along else outs[0]


def _merge_body(r_ref, a_ref, wr_ref, wa_ref, gr_ref, ga_ref, o_ref):
    wr = wr_ref[...].astype(BF16)
    wa = wa_ref[...].astype(BF16)
    for c in range(o_ref.shape[0] // MERGE_ROW_CHUNK):
        rs = slice(c * MERGE_ROW_CHUNK, (c + 1) * MERGE_ROW_CHUNK)
        gr = jax.nn.sigmoid(gr_ref[rs, :].astype(F32))
        ga = jax.nn.sigmoid(ga_ref[rs, :].astype(F32))
        o_ref[rs, :] = (gr * _dot(r_ref[rs, :], wr) + ga * _dot(a_ref[rs, :], wa)).astype(o_ref.dtype)


def _ffn_up_body(h_ref, wa_ref, wb_ref, o_ref):
    wa = wa_ref[...]
    wb = wb_ref[...]
    for c in range(h_ref.shape[0] // FFN_ROW_CHUNK):
        rs = slice(c * FFN_ROW_CHUNK, (c + 1) * FFN_ROW_CHUNK)
        h = h_ref[rs, :]
        o_ref[rs, :] = (jax.nn.silu(_dot(h, wa)) * _dot(h, wb)).astype(o_ref.dtype)


def _decay_body(lc, ldf_ref, ldb_ref, d_ref, qf_ref, qb_ref, kf_ref, kb_ref):
    h = pl.program_id(0)
    r = pl.program_id(1)
    rb, l = d_ref.shape

    def log_gamma(ld_ref, shape):
        return -jnp.exp(jnp.full(shape, ld_ref[h], F32))

    def rows(shape):
        return (lax.broadcasted_iota(jnp.int32, shape, 0) + r * rb).astype(F32)

    def cols(shape):
        return lax.broadcasted_iota(jnp.int32, shape, 1).astype(F32)

    diff = rows((rb, l)) - cols((rb, l))
    d_f = jnp.where(diff >= 0, jnp.exp(log_gamma(ldf_ref, (rb, l)) * jnp.maximum(diff, 0.0)), 0.0)
    d_b = jnp.where(diff <= 0, jnp.exp(log_gamma(ldb_ref, (rb, l)) * jnp.maximum(-diff, 0.0)), 0.0)
    d_ref[...] = d_f + d_b
    qs = qf_ref.shape
    qf_ref[...] = jnp.exp(log_gamma(ldf_ref, qs) * (rows(qs) + 1.0))
    qb_ref[...] = jnp.exp(log_gamma(ldb_ref, qs) * (float(l) - rows(qs)))
    ks = kf_ref.shape
    kf_ref[...] = jnp.exp(log_gamma(ldf_ref, ks) * (float(lc - 1) - cols(ks)))
    kb_ref[...] = jnp.exp(log_gamma(ldb_ref, ks) * cols(ks))


def _decay_tables(ldf, ldb, l, lc, dk):
    rb = 256
    return pl.pallas_call(
        functools.partial(_decay_body, lc),
        out_shape=(jax.ShapeDtypeStruct((RET_HEADS, l, l), F32),
                   jax.ShapeDtypeStruct((RET_HEADS, l, dk), F32),
                   jax.ShapeDtypeStruct((RET_HEADS, l, dk), F32),
                   jax.ShapeDtypeStruct((RET_HEADS, 8, lc), F32),
                   jax.ShapeDtypeStruct((RET_HEADS, 8, lc), F32)),
        grid=(RET_HEADS, l // rb),
        in_specs=[pl.BlockSpec(memory_space=pltpu.SMEM), pl.BlockSpec(memory_space=pltpu.SMEM)],
        out_specs=(pl.BlockSpec((None, rb, l), lambda h, r: (h, r, 0)),
                   pl.BlockSpec((None, rb, dk), lambda h, r: (h, r, 0)),
                   pl.BlockSpec((None, rb, dk), lambda h, r: (h, r, 0)),
                   pl.BlockSpec((None, 8, lc), lambda h, r: (h, 0, 0)),
                   pl.BlockSpec((None, 8, lc), lambda h, r: (h, 0, 0))),
        compiler_params=_params(2),
        name="decay_tables",
    )(ldf, ldb)


def _gated_group_norm(o, g_bf16, gn):
    on = o * _rms_scale(o) * gn
    return jax.nn.silu(g_bf16.astype(F32)) * on


def _ctx_or_fill(n_ctx, o_ref, compute):
    b = pl.program_id(0)
    pl.when(b < n_ctx)(compute)

    @pl.when(b >= n_ctx)
    def _():
        o_ref[...] = jnp.zeros(o_ref.shape, o_ref.dtype)


def _ret_ctx_body(dk, n_ctx, q_ref, k_ref, v_ref, g_ref, d_ref, kf_ref, kb_ref, gn_ref, o_ref, sf_ref, sb_ref):
    def compute():
        for h in range(RET_HEADS):
            sl = slice(h * dk, (h + 1) * dk)
            q = q_ref[:, sl]
            k32 = k_ref[:, sl].astype(F32) * (dk ** -0.5)
            v = v_ref[:, sl]
            p = (_dot_nt(q, k32.astype(BF16)) * d_ref[h]).astype(BF16)
            o = _dot(p, v)
            kt = k32.T
            sf_ref[h] = _dot((kt * kf_ref[h][0:1, :]).astype(BF16), v)
            sb_ref[h] = _dot((kt * kb_ref[h][0:1, :]).astype(BF16), v)
            o_ref[:, sl] = _gated_group_norm(o, g_ref[:, sl], gn_ref[:, sl]).astype(o_ref.dtype)

    _ctx_or_fill(n_ctx, o_ref, compute)


def _ret_lat_body(dk, rb, buf_ref, q_ref, k_ref, v_ref, g_ref, d_ref, qf_ref, qb_ref, cos_ref, sin_ref,
                  s0f_ref, s0b_ref, gn_ref, o_ref):
    del buf_ref
    cos = cos_ref[...]
    sin = sin_ref[...]

    def rope(x):
        swapped = jnp.concatenate(
            [pltpu.roll(x[:, c * LANES:(c + 1) * LANES], LANES // 2, 1) for c in range(dk // LANES)], axis=1)
        return x * cos + swapped * sin

    q = rope(q_ref[...].astype(F32)).astype(BF16)
    k = rope(k_ref[...].astype(F32) * (dk ** -0.5)).astype(BF16)
    v = v_ref[...]
    s0f = s0f_ref[...].astype(BF16)
    s0b = s0b_ref[...].astype(BF16)
    gn = gn_ref[...]
    for r in range(q.shape[0] // rb):
        rs = slice(r * rb, (r + 1) * rb)
        qr = q[rs]
        p = (_dot_nt(qr, k) * d_ref[rs, :]).astype(BF16)
        o = _dot(p, v) + qf_ref[rs, :] * _dot(qr, s0f) + qb_ref[rs, :] * _dot(qr, s0b)
        o_ref[rs, :] = _gated_group_norm(o, g_ref[rs, :], gn).astype(o_ref.dtype)


def _softmax_t(scores, sink_row):
    m = sink_row
    for s in scores:
        m = jnp.maximum(m, jnp.max(s, axis=0, keepdims=True))
    ps = [jnp.exp(s - m) for s in scores]
    denom = jnp.exp(sink_row - m)
    for p in ps:
        denom = denom + jnp.sum(p, axis=0, keepdims=True)
    return [p.astype(BF16) for p in ps], denom


def _sink_row(sink_ref, kh, queries):
    return jnp.concatenate(
        [jnp.full((1, queries), sink_ref[kh * ATT_GROUPS + g], F32) for g in range(ATT_GROUPS)], axis=1)


def _att_ctx_body(hd, n_ctx, sink_ref, q_ref, k_ref, v_ref, o_ref, nk_ref, nv_ref):
    rows = q_ref.shape[0]
    scale = hd ** -0.5

    def compute():
        for kh in range(ATT_KV_HEADS):
            cs = slice(kh * hd, (kh + 1) * hd)
            heads = [kh * ATT_GROUPS + g for g in range(ATT_GROUPS)]
            qg = jnp.concatenate([q_ref[:, hh * hd:(hh + 1) * hd] for hh in heads], axis=0)
            k32 = k_ref[:, cs]
            v32 = v_ref[:, cs]
            nk_ref[pl.ds(kh, rows, stride=ATT_KV_HEADS), :] = k32
            nv_ref[pl.ds(kh, rows, stride=ATT_KV_HEADS), :] = v32
            k = (k32 * scale).astype(BF16)
            (p,), denom = _softmax_t([_dot_nt(k, qg)], _sink_row(sink_ref, kh, rows))
            o_t = _dot(v32.T.astype(BF16), p) / denom
            for g, hh in enumerate(heads):
                o_ref[:, hh * hd:(hh + 1) * hd] = o_t[:, g * rows:(g + 1) * rows].T.astype(o_ref.dtype)

    _ctx_or_fill(n_ctx, o_ref, compute)


def _att_lat_body(hd, length, sink_ref, buf_ref, q_ref, kp_ref, kc_ref, kn_ref, vp_ref, vc_ref, vn_ref,
                  ck_ref, cv_ref, cos_ref, sa_ref, sb_ref, o_ref):
    del buf_ref
    n = pl.program_id(1)
    nb = length // BLOCK
    scale = hd ** -0.5

    def rope(x, blk):
        rows = pl.ds(pl.multiple_of(blk * BLOCK, BLOCK), BLOCK)
        return (x * cos_ref[rows, :] + pltpu.roll(x, 3 * hd // 4, 1) * sa_ref[rows, :]
                + pltpu.roll(x, hd // 4, 1) * sb_ref[rows, :])

    bp = jnp.maximum(n - 1, 0)
    bn = jnp.minimum(n + 1, nb - 1)
    kpos = (n - 1) * BLOCK + lax.broadcasted_iota(jnp.int32, (3 * BLOCK, BLOCK), 0)
    qpos = n * BLOCK + lax.broadcasted_iota(jnp.int32, (3 * BLOCK, BLOCK), 1)
    mask = (jnp.abs(kpos - qpos) <= WINDOW) & (kpos >= 0) & (kpos < length)
    mask = jnp.concatenate([mask] * ATT_GROUPS, axis=1)
    for kh in range(ATT_KV_HEADS):
        cs = slice(kh * hd, (kh + 1) * hd)
        heads = [kh * ATT_GROUPS + g for g in range(ATT_GROUPS)]
        kw = (jnp.concatenate(
            [rope(kp_ref[:, cs], bp), rope(kc_ref[:, cs], n), rope(kn_ref[:, cs], bn)], axis=0)
              * scale).astype(BF16)
        vw_t = jnp.concatenate([vp_ref[:, cs], vc_ref[:, cs], vn_ref[:, cs]], axis=0).T.astype(BF16)
        qg = jnp.concatenate(
            [rope(q_ref[:, hh * hd:(hh + 1) * hd].astype(F32), n) for hh in heads], axis=0).astype(BF16)
        past = ck_ref.shape[0] // ATT_KV_HEADS
        ck = ck_ref[pl.ds(kh, past, stride=ATT_KV_HEADS), :]
        cv = cv_ref[pl.ds(kh, past, stride=ATT_KV_HEADS), :]
        sw = jnp.where(mask, _dot_nt(kw, qg), NEG_INF)
        sc = _dot_nt((ck * scale).astype(BF16), qg)
        (pw, pc), denom = _softmax_t([sw, sc], _sink_row(sink_ref, kh, BLOCK))
        o_t = (_dot(vw_t, pw) + _dot(cv.T.astype(BF16), pc)) / denom
        for g, hh in enumerate(heads):
            o_ref[:, hh * hd:(hh + 1) * hd] = o_t[:, g * BLOCK:(g + 1) * BLOCK].T.astype(o_ref.dtype)


def _rope_tables(length, hd):
    half = hd // 2
    quarter = half // 2
    rows = length // GRID_W
    row = jnp.repeat(jnp.arange(rows), GRID_W).astype(F32)
    col = jnp.tile(jnp.arange(GRID_W), rows).astype(F32)
    inv_freq = ROPE_BASE ** (-jnp.arange(quarter, dtype=F32) / quarter)
    ang_r = row[:, None] * inv_freq[None, :]
    ang_c = col[:, None] * inv_freq[None, :]
    zeros = jnp.zeros_like(ang_r)
    cos = jnp.concatenate([jnp.cos(ang_r)] * 2 + [jnp.cos(ang_c)] * 2, axis=-1)
    sin_hi = jnp.concatenate([-jnp.sin(ang_r), zeros, -jnp.sin(ang_c), zeros], axis=-1)
    sin_lo = jnp.concatenate([zeros, jnp.sin(ang_r), zeros, jnp.sin(ang_c)], axis=-1)
    return cos, sin_hi, sin_lo


def kernel(x_prompt, x_sample, cache_att_k, cache_att_v, state_ret_fwd, state_ret_bwd, c, c_ctx, w_mod, b_mod, norm_mix_pre, norm_mix_post, norm_ffn_pre, norm_ffn_post, w_in, ret_log_decay_fwd, ret_log_decay_bwd, ret_gn_w, att_sink, w_branch_ret, w_branch_att, w_out, w_ffn_up, w_ffn_down):
    nbc, lc, d = x_prompt.shape
    nbl, ll, _ = x_sample.shape
    depth = w_mod.shape[0]
    assert depth == 1
    tc = nbc * lc
    tl = nbl * ll
    t = tc + tl
    ret_w = w_branch_ret.shape[1]
    att_w = w_branch_att.shape[1]
    dk = ret_w // RET_HEADS
    hd = att_w // ATT_HEADS
    kv_w = ATT_KV_HEADS * hd
    hidden = w_ffn_down.shape[1]
    past = cache_att_k.shape[2]
    assert ret_w == att_w and ll % GRID_W == 0

    xp = x_prompt.reshape(tc, d)
    xs = x_sample.reshape(tl, d)
    w_in2 = w_in.reshape(d, -1)

    groups = 8
    assert 1 + nbl <= groups
    cond = jnp.concatenate([c_ctx[None, :], c, jnp.zeros((groups - 1 - nbl, d), F32)], axis=0)
    mods = _adaln(cond, w_mod.reshape(d, 6 * d), b_mod.reshape(1, 6 * d)).reshape(groups, 6, 1, d)

    def group_of(tr):
        return lambda i: jnp.where(i * tr < tc, 0, 1 + (i * tr - tc) // ll)

    def vec(a):
        return a.reshape(1, d)

    def row_spec(tr, width, off=0):
        return pl.BlockSpec((tr, width), lambda i: (i + off, 0))

    def x_specs(tr):
        nct = tc // tr
        return [pl.BlockSpec((tr, d), lambda i: (jnp.minimum(i, nct - 1), 0)),
                pl.BlockSpec((tr, d), lambda i: (jnp.maximum(i - nct, 0), 0))]

    vec_spec = pl.BlockSpec((1, d), lambda i: (0, 0))

    tr = 256
    g = group_of(tr)
    h = pl.pallas_call(
        functools.partial(_prenorm_body, tc // tr),
        out_shape=jax.ShapeDtypeStruct((t, d), BF16),
        grid=(t // tr,),
        in_specs=x_specs(tr) + [vec_spec, _mods_spec(d, 0, g), _mods_spec(d, 1, g)],
        out_specs=row_spec(tr, d),
        compiler_params=_params(1),
        name="prenorm",
    )(xp, xs, vec(norm_mix_pre), mods, mods)

    tm, tn = 1024, 512
    tm_big = _row_tile(t, (1536, 1024))
    kv_off = 4 * ret_w + att_w
    gate_off = kv_off + 2 * kv_w
    n_front = kv_off // tn
    w_up = w_ffn_up.reshape(d, 2 * hidden)
    w_down = w_ffn_down.reshape(hidden, d)
    proj, w_up16, w_down16 = _matmul(
        h, w_in2, n_out=kv_off + 2 * d,
        w_col_block=lambda j: jnp.where(j < n_front, j, j + (gate_off - kv_off) // tn),
        out_dtype=BF16, tm=tm_big, tn=tn, name="in_proj",
        cast_along=((w_up, 4), (w_down, 16)))
    kv = _matmul(h, w_in2, n_out=2 * kv_w, w_col_block=lambda j: j + kv_off // tn,
                 out_dtype=F32, tm=tm_big, tn=tn, name="in_proj_kv")

    d_tab, qf_tab, qb_tab, kf_tab, kb_tab = _decay_tables(
        ret_log_decay_fwd.reshape(RET_HEADS), ret_log_decay_bwd.reshape(RET_HEADS), ll, lc, dk)
    gn = ret_gn_w.reshape(1, ret_w)
    ctx_steps = t // lc

    def cb(b):
        return jnp.minimum(b, nbc - 1)

    state_spec = pl.BlockSpec((None, None, RET_HEADS, dk, dk), lambda b: (cb(b), 0, 0, 0, 0))
    ret_ctx, s_f, s_b = pl.pallas_call(
        functools.partial(_ret_ctx_body, dk, nbc),
        out_shape=(jax.ShapeDtypeStruct((t, ret_w), BF16),
                   jax.ShapeDtypeStruct((nbc, 1, RET_HEADS, dk, dk), F32),
                   jax.ShapeDtypeStruct((nbc, 1, RET_HEADS, dk, dk), F32)),
        grid=(ctx_steps,),
        in_specs=[pl.BlockSpec((lc, ret_w), lambda b: (cb(b), 0)),
                  pl.BlockSpec((lc, ret_w), lambda b: (cb(b), 1)),
                  pl.BlockSpec((lc, ret_w), lambda b: (cb(b), 2)),
                  pl.BlockSpec((lc, ret_w), lambda b: (cb(b), 3)),
                  pl.BlockSpec((RET_HEADS, lc, lc), lambda b: (0, 0, 0)),
                  pl.BlockSpec((RET_HEADS, 8, lc), lambda b: (0, 0, 0)),
                  pl.BlockSpec((RET_HEADS, 8, lc), lambda b: (0, 0, 0)),
                  pl.BlockSpec((1, ret_w), lambda b: (0, 0))],
        out_specs=(pl.BlockSpec((lc, ret_w), lambda b: (b, 0)), state_spec, state_spec),
        compiler_params=_params(1),
        name="ret_ctx",
    )(proj, proj, proj, proj, d_tab, kf_tab, kb_tab, gn)

    cos_r, sin_hi_r, sin_lo_r = _rope_tables(ll, dk)
    rb0 = tc // ll
    hb = ret_w // dk
    ret_out = pl.pallas_call(
        functools.partial(_ret_lat_body, dk, 256),
        out_shape=jax.ShapeDtypeStruct((t, ret_w), BF16),
        grid=(RET_HEADS, nbl),
        in_specs=[pl.BlockSpec(memory_space=pl.ANY),
                  pl.BlockSpec((ll, dk), lambda hh, b: (rb0 + b, hh)),
                  pl.BlockSpec((ll, dk), lambda hh, b: (rb0 + b, hb + hh)),
                  pl.BlockSpec((ll, dk), lambda hh, b: (rb0 + b, 2 * hb + hh)),
                  pl.BlockSpec((ll, dk), lambda hh, b: (rb0 + b, 3 * hb + hh)),
                  pl.BlockSpec((None, ll, ll), lambda hh, b: (hh, 0, 0)),
                  pl.BlockSpec((None, ll, dk), lambda hh, b: (hh, 0, 0)),
                  pl.BlockSpec((None, ll, dk), lambda hh, b: (hh, 0, 0)),
                  pl.BlockSpec((ll, dk), lambda hh, b: (0, 0)),
                  pl.BlockSpec((ll, dk), lambda hh, b: (0, 0)),
                  pl.BlockSpec((None, None, None, dk, dk), lambda hh, b: (b, 0, hh, 0, 0)),
                  pl.BlockSpec((None, None, None, dk, dk), lambda hh, b: (b, 0, hh, 0, 0)),
                  pl.BlockSpec((1, dk), lambda hh, b: (0, hh))],
        out_specs=pl.BlockSpec((ll, dk), lambda hh, b: (rb0 + b, hh)),
        input_output_aliases={0: 0},
        compiler_params=_params(2),
        name="ret_lat",
    )(ret_ctx, proj, proj, proj, proj, d_tab, qf_tab, qb_tab, cos_r, sin_hi_r + sin_lo_r,
      state_ret_fwd, state_ret_bwd, gn)

    sink = att_sink.reshape(ATT_HEADS)
    aq_blk = 4 * ret_w // att_w
    new_kv_spec = pl.BlockSpec((lc * ATT_KV_HEADS, hd), lambda b: (cb(b), 0))
    att_ctx, new_k, new_v = pl.pallas_call(
        functools.partial(_att_ctx_body, hd, nbc),
        out_shape=(jax.ShapeDtypeStruct((t, att_w), BF16),
                   jax.ShapeDtypeStruct((tc * ATT_KV_HEADS, hd), F32),
                   jax.ShapeDtypeStruct((tc * ATT_KV_HEADS, hd), F32)),
        grid=(ctx_steps,),
        in_specs=[pl.BlockSpec(memory_space=pltpu.SMEM),
                  pl.BlockSpec((lc, att_w), lambda b: (cb(b), aq_blk)),
                  pl.BlockSpec((lc, kv_w), lambda b: (cb(b), 0)),
                  pl.BlockSpec((lc, kv_w), lambda b: (cb(b), 1))],
        out_specs=(pl.BlockSpec((lc, att_w), lambda b: (b, 0)), new_kv_spec, new_kv_spec),
        compiler_params=_params(1),
        name="att_ctx",
    )(sink, proj, kv, kv)
    new_k = new_k.reshape(nbc, 1, lc, ATT_KV_HEADS, hd)
    new_v = new_v.reshape(nbc, 1, lc, ATT_KV_HEADS, hd)

    cos_a, sin_hi_a, sin_lo_a = _rope_tables(ll, hd)
    nqb = ll // BLOCK
    qb0 = tc // BLOCK

    def kv_spec(shift, col):
        return pl.BlockSpec(
            (BLOCK, kv_w), lambda b, n: (qb0 + b * nqb + jnp.clip(n + shift, 0, nqb - 1), col))

    tab_spec = pl.BlockSpec((ll, hd), lambda b, n: (0, 0))
    att_out = pl.pallas_call(
        functools.partial(_att_lat_body, hd, ll),
        out_shape=jax.ShapeDtypeStruct((t, att_w), BF16),
        grid=(nbl, nqb),
        in_specs=[pl.BlockSpec(memory_space=pltpu.SMEM),
                  pl.BlockSpec(memory_space=pl.ANY),
                  pl.BlockSpec((BLOCK, att_w), lambda b, n: (qb0 + b * nqb + n, aq_blk)),
                  kv_spec(-1, 0), kv_spec(0, 0), kv_spec(1, 0),
                  kv_spec(-1, 1), kv_spec(0, 1), kv_spec(1, 1),
                  pl.BlockSpec((past * ATT_KV_HEADS, hd), lambda b, n: (b, 0)),
                  pl.BlockSpec((past * ATT_KV_HEADS, hd), lambda b, n: (b, 0)),
                  tab_spec, tab_spec, tab_spec],
        out_specs=pl.BlockSpec((BLOCK, att_w), lambda b, n: (qb0 + b * nqb + n, 0)),
        input_output_aliases={1: 0},
        compiler_params=_params(2),
        name="att_lat",
    )(sink, att_ctx, proj, kv, kv, kv, kv, kv, kv,
      cache_att_k.reshape(nbl * past * ATT_KV_HEADS, hd), cache_att_v.reshape(nbl * past * ATT_KV_HEADS, hd),
      cos_a, sin_hi_a, sin_lo_a)

    gr_blk = kv_off // tn
    mix_pre = pl.pallas_call(
        _merge_body,
        out_shape=jax.ShapeDtypeStruct((t, d), BF16),
        grid=(d // tn, t // tm),
        in_specs=[pl.BlockSpec((tm, ret_w), lambda j, i: (i, 0)),
                  pl.BlockSpec((tm, att_w), lambda j, i: (i, 0)),
                  pl.BlockSpec((ret_w, tn), lambda j, i: (0, j)),
                  pl.BlockSpec((att_w, tn), lambda j, i: (0, j)),
                  pl.BlockSpec((tm, tn), lambda j, i: (i, gr_blk + j)),
                  pl.BlockSpec((tm, tn), lambda j, i: (i, gr_blk + d // tn + j))],
        out_specs=pl.BlockSpec((tm, tn), lambda j, i: (i, j)),
        compiler_params=_params(2),
        name="merge",
    )(ret_out, att_out, w_branch_ret.reshape(ret_w, d), w_branch_att.reshape(att_w, d), proj, proj)

    mix = _matmul(mix_pre, w_out.reshape(d, d), n_out=d, w_col_block=lambda j: j,
                  out_dtype=BF16, tm=tm_big, tn=tn, name="out_proj")

    tr = 256
    g = group_of(tr)
    h2 = pl.pallas_call(
        functools.partial(_post1_body, tc // tr),
        out_shape=jax.ShapeDtypeStruct((t, d), BF16),
        grid=(t // tr,),
        in_specs=x_specs(tr) + [row_spec(tr, d), _mods_spec(d, 2, g), vec_spec, vec_spec,
                                _mods_spec(d, 3, g), _mods_spec(d, 4, g)],
        out_specs=row_spec(tr, d),
        compiler_params=_params(1),
        name="post_mix",
    )(xp, xs, mix, mods, vec(norm_mix_post), vec(norm_ffn_pre), mods, mods)

    tnh = 256
    tm_ffn = _row_tile(t, (2048, 1024))
    n_j = hidden // tnh
    act = pl.pallas_call(
        _ffn_up_body,
        out_shape=jax.ShapeDtypeStruct((t, hidden), BF16),
        grid=(n_j, t // tm_ffn),
        in_specs=[pl.BlockSpec((tm_ffn, d), lambda j, i: (i, 0)),
                  pl.BlockSpec((d, tnh), lambda j, i: (0, j)),
                  pl.BlockSpec((d, tnh), lambda j, i: (0, n_j + j))],
        out_specs=pl.BlockSpec((tm_ffn, tnh), lambda j, i: (i, j)),
        compiler_params=_params(2),
        name="ffn_up",
    )(h2, w_up16, w_up16)
    f = _matmul(act, w_down16, n_out=d, w_col_block=lambda j: j, out_dtype=BF16, tm=512, tn=tn,
                name="ffn_down")

    def final(x, off_tiles, name):
        rows = x.shape[0]
        gg = group_of(tr)

        def mod_spec(m):
            return pl.BlockSpec((None, None, 1, d), lambda i: (gg(i + off_tiles), m, 0, 0))

        return pl.pallas_call(
            _final_body,
            out_shape=jax.ShapeDtypeStruct((rows, d), F32),
            grid=(rows // tr,),
            in_specs=[row_spec(tr, d), row_spec(tr, d, off_tiles), row_spec(tr, d, off_tiles),
                      mod_spec(2), vec_spec, mod_spec(5), vec_spec],
            out_specs=row_spec(tr, d),
            compiler_params=_params(1),
            name=name,
        )(x, mix, f, mods, vec(norm_mix_post), mods, vec(norm_ffn_post))

    y_prompt = final(xp, 0, "final_ctx").reshape(nbc, lc, d)
    y_sample = final(xs, tc // tr, "final_lat").reshape(nbl, ll, d)

    return y_prompt, y_sample, new_k, new_v, s_f, s_b
```

```python
import functools

import jax
import jax.numpy as jnp
from jax import lax
from jax.experimental import pallas as pl
from jax.experimental.pallas import tpu as pltpu

F32 = jnp.float32
BF16 = jnp.bfloat16

GRID_W = 64
RET_HEADS = 8
ATT_HEADS = 16
ATT_KV_HEADS = 4
ATT_GROUPS = ATT_HEADS // ATT_KV_HEADS
WINDOW = 128
BLOCK = 128
ROPE_BASE = 10000.0
NORM_EPS = 1e-6
NEG_INF = -1e30

VMEM_LIMIT_BYTES = 56 * 1024 * 1024
LANES = 128
FFN_ROW_CHUNK = 512
MERGE_ROW_CHUNK = 512


def _params(n_axes):
    return pltpu.CompilerParams(
        dimension_semantics=("arbitrary",) * n_axes, vmem_limit_bytes=VMEM_LIMIT_BYTES)


def _row_tile(rows, candidates):
    return next(c for c in candidates if rows % c == 0)


def _rms_scale(x):
    return lax.rsqrt(jnp.mean(x * x, axis=-1, keepdims=True) + NORM_EPS)


def _dot(a, b):
    return jnp.dot(a, b, preferred_element_type=F32)


def _dot_nt(a, b):
    return lax.dot_general(a, b, (((1,), (1,)), ((), ())), preferred_element_type=F32)


def _adaln_body(c_ref, w_ref, b_ref, o_ref):
    o_ref[...] = _dot(jax.nn.silu(c_ref[...]), w_ref[...]) + b_ref[...]


def _adaln(cond, w_mod, b_mod, tn=512):
    rows, d = cond.shape
    n = w_mod.shape[1]
    return pl.pallas_call(
        _adaln_body,
        out_shape=jax.ShapeDtypeStruct((rows, n), F32),
        grid=(n // tn,),
        in_specs=[pl.BlockSpec((rows, d), lambda j: (0, 0)),
                  pl.BlockSpec((d, tn), lambda j: (0, j)),
                  pl.BlockSpec((1, tn), lambda j: (0, j))],
        out_specs=pl.BlockSpec((rows, tn), lambda j: (0, j)),
        compiler_params=_params(1),
        name="adaln",
    )(cond, w_mod, b_mod)


def _mods_spec(d, m, group_of):
    return pl.BlockSpec((None, None, 1, d), lambda i: (group_of(i), m, 0, 0))


def _prenorm_body(nc_tiles, xp_ref, xs_ref, w_ref, sh_ref, sc_ref, h_ref):
    def run(x_ref):
        x = x_ref[...]
        y = x * _rms_scale(x) * w_ref[...]
        h_ref[...] = (y * (1.0 + sc_ref[...]) + sh_ref[...]).astype(h_ref.dtype)

    i = pl.program_id(0)
    pl.when(i < nc_tiles)(lambda: run(xp_ref))
    pl.when(i >= nc_tiles)(lambda: run(xs_ref))


def _residual1(x_ref, mix_ref, g1_ref, wpost_ref):
    m = mix_ref[...].astype(F32)
    return x_ref[...] + g1_ref[...] * (m * _rms_scale(m) * wpost_ref[...])


def _post1_body(nc_tiles, xp_ref, xs_ref, mix_ref, g1_ref, wpost_ref, wpre_ref, sh_ref, sc_ref, h2_ref):
    def run(x_ref):
        x1 = _residual1(x_ref, mix_ref, g1_ref, wpost_ref)
        y = x1 * _rms_scale(x1) * wpre_ref[...]
        h2_ref[...] = (y * (1.0 + sc_ref[...]) + sh_ref[...]).astype(h2_ref.dtype)

    i = pl.program_id(0)
    pl.when(i < nc_tiles)(lambda: run(xp_ref))
    pl.when(i >= nc_tiles)(lambda: run(xs_ref))


def _final_body(x_ref, mix_ref, f_ref, g1_ref, wpost_ref, g2_ref, w_ref, y_ref):
    x1 = _residual1(x_ref, mix_ref, g1_ref, wpost_ref)
    f = f_ref[...].astype(F32)
    y_ref[...] = x1 + g2_ref[...] * (f * _rms_scale(f) * w_ref[...])


def _mm_body(n_cast, x_ref, w_ref, *refs):
    o_ref = refs[n_cast]
    o_ref[...] = _dot(x_ref[...], w_ref[...].astype(BF16)).astype(o_ref.dtype)
    for src_ref, dst_ref in zip(refs[:n_cast], refs[n_cast + 1:]):
        dst_ref[...] = src_ref[...].astype(dst_ref.dtype)


BF16_SUBLANES = 16


def _cast_along_spec(src, n_i, steps):
    units = src.shape[0] // BF16_SUBLANES
    n_blocks = max(b for b in range(1, min(units, steps) + 1) if units % b == 0)
    return pl.BlockSpec((src.shape[0] // n_blocks, src.shape[1]),
                        lambda j, i: (jnp.minimum(j * n_i + i, n_blocks - 1), 0))


def _matmul(x, w, *, n_out, w_col_block, out_dtype, tm, tn, name, cast_along=()):
    m, k = x.shape
    n_i = m // tm
    steps = (n_out // tn) * n_i
    cast_specs = [_cast_along_spec(src, n_i, steps) for src in cast_along]
    cast_shapes = [jax.ShapeDtypeStruct(src.shape, BF16) for src in cast_along]
    outs = pl.pallas_call(
        functools.partial(_mm_body, len(cast_along)),
        out_shape=[jax.ShapeDtypeStruct((m, n_out), out_dtype)] + cast_shapes,
        grid=(n_out // tn, n_i),
        in_specs=[pl.BlockSpec((tm, k), lambda j, i: (i, 0)),
                  pl.BlockSpec((k, tn), lambda j, i: (0, w_col_block(j)))] + cast_specs,
        out_specs=[pl.BlockSpec((tm, tn), lambda j, i: (i, j))] + cast_specs,
        compiler_params=_params(2),
        name=name,
    )(x, w, *cast_along)
    return outs if cast_along else outs[0]


def _merge_body(r_ref, a_ref, wr_ref, wa_ref, gr_ref, ga_ref, o_ref):
    wr = wr_ref[...].astype(BF16)
    wa = wa_ref[...].astype(BF16)
    for c in range(o_ref.shape[0] // MERGE_ROW_CHUNK):
        rs = slice(c * MERGE_ROW_CHUNK, (c + 1) * MERGE_ROW_CHUNK)
        gr = jax.nn.sigmoid(gr_ref[rs, :].astype(F32))
        ga = jax.nn.sigmoid(ga_ref[rs, :].astype(F32))
        o_ref[rs, :] = (gr * _dot(r_ref[rs, :], wr) + ga * _dot(a_ref[rs, :], wa)).astype(o_ref.dtype)


def _ffn_up_body(h_ref, wa_ref, wb_ref, wd_ref, o_ref, wd16_ref):
    wa = wa_ref[...]
    wb = wb_ref[...]
    for c in range(h_ref.shape[0] // FFN_ROW_CHUNK):
        rs = slice(c * FFN_ROW_CHUNK, (c + 1) * FFN_ROW_CHUNK)
        h = h_ref[rs, :]
        o_ref[rs, :] = (jax.nn.silu(_dot(h, wa)) * _dot(h, wb)).astype(o_ref.dtype)
    wd16_ref[...] = wd_ref[...].astype(wd16_ref.dtype)


def _decay_body(lc, ldf_ref, ldb_ref, d_ref, qf_ref, qb_ref, kf_ref, kb_ref):
    h = pl.program_id(0)
    r = pl.program_id(1)
    rb, l = d_ref.shape

    def log_gamma(ld_ref, shape):
        return -jnp.exp(jnp.full(shape, ld_ref[h], F32))

    def rows(shape):
        return (lax.broadcasted_iota(jnp.int32, shape, 0) + r * rb).astype(F32)

    def cols(shape):
        return lax.broadcasted_iota(jnp.int32, shape, 1).astype(F32)

    diff = rows((rb, l)) - cols((rb, l))
    d_f = jnp.where(diff >= 0, jnp.exp(log_gamma(ldf_ref, (rb, l)) * jnp.maximum(diff, 0.0)), 0.0)
    d_b = jnp.where(diff <= 0, jnp.exp(log_gamma(ldb_ref, (rb, l)) * jnp.maximum(-diff, 0.0)), 0.0)
    d_ref[...] = d_f + d_b
    qs = qf_ref.shape
    qf_ref[...] = jnp.exp(log_gamma(ldf_ref, qs) * (rows(qs) + 1.0))
    qb_ref[...] = jnp.exp(log_gamma(ldb_ref, qs) * (float(l) - rows(qs)))
    ks = kf_ref.shape
    kf_ref[...] = jnp.exp(log_gamma(ldf_ref, ks) * (float(lc - 1) - cols(ks)))
    kb_ref[...] = jnp.exp(log_gamma(ldb_ref, ks) * cols(ks))


def _decay_tables(ldf, ldb, l, lc, dk):
    rb = 256
    return pl.pallas_call(
        functools.partial(_decay_body, lc),
        out_shape=(jax.ShapeDtypeStruct((RET_HEADS, l, l), F32),
                   jax.ShapeDtypeStruct((RET_HEADS, l, dk), F32),
                   jax.ShapeDtypeStruct((RET_HEADS, l, dk), F32),
                   jax.ShapeDtypeStruct((RET_HEADS, 8, lc), F32),
                   jax.ShapeDtypeStruct((RET_HEADS, 8, lc), F32)),
        grid=(RET_HEADS, l // rb),
        in_specs=[pl.BlockSpec(memory_space=pltpu.SMEM), pl.BlockSpec(memory_space=pltpu.SMEM)],
        out_specs=(pl.BlockSpec((None, rb, l), lambda h, r: (h, r, 0)),
                   pl.BlockSpec((None, rb, dk), lambda h, r: (h, r, 0)),
                   pl.BlockSpec((None, rb, dk), lambda h, r: (h, r, 0)),
                   pl.BlockSpec((None, 8, lc), lambda h, r: (h, 0, 0)),
                   pl.BlockSpec((None, 8, lc), lambda h, r: (h, 0, 0))),
        compiler_params=_params(2),
        name="decay_tables",
    )(ldf, ldb)


def _gated_group_norm(o, g_bf16, gn):
    on = o * _rms_scale(o) * gn
    return jax.nn.silu(g_bf16.astype(F32)) * on


def _ctx_or_fill(n_ctx, o_ref, compute):
    b = pl.program_id(0)
    pl.when(b < n_ctx)(compute)

    @pl.when(b >= n_ctx)
    def _():
        o_ref[...] = jnp.zeros(o_ref.shape, o_ref.dtype)


def _ret_ctx_body(dk, n_ctx, q_ref, k_ref, v_ref, g_ref, d_ref, kf_ref, kb_ref, gn_ref, o_ref, sf_ref, sb_ref):
    def compute():
        for h in range(RET_HEADS):
            sl = slice(h * dk, (h + 1) * dk)
            q = q_ref[:, sl]
            k32 = k_ref[:, sl].astype(F32) * (dk ** -0.5)
            v = v_ref[:, sl]
            p = (_dot_nt(q, k32.astype(BF16)) * d_ref[h]).astype(BF16)
            o = _dot(p, v)
            kt = k32.T
            sf_ref[h] = _dot((kt * kf_ref[h][0:1, :]).astype(BF16), v)
            sb_ref[h] = _dot((kt * kb_ref[h][0:1, :]).astype(BF16), v)
            o_ref[:, sl] = _gated_group_norm(o, g_ref[:, sl], gn_ref[:, sl]).astype(o_ref.dtype)

    _ctx_or_fill(n_ctx, o_ref, compute)


def _ret_lat_body(dk, rb, buf_ref, q_ref, k_ref, v_ref, g_ref, d_ref, qf_ref, qb_ref, cos_ref, sin_ref,
                  s0f_ref, s0b_ref, gn_ref, o_ref):
    del buf_ref
    cos = cos_ref[...]
    sin = sin_ref[...]

    def rope(x):
        swapped = jnp.concatenate(
            [pltpu.roll(x[:, c * LANES:(c + 1) * LANES], LANES // 2, 1) for c in range(dk // LANES)], axis=1)
        return x * cos + swapped * sin

    q = rope(q_ref[...].astype(F32)).astype(BF16)
    k = rope(k_ref[...].astype(F32) * (dk ** -0.5)).astype(BF16)
    v = v_ref[...]
    s0f = s0f_ref[...].astype(BF16)
    s0b = s0b_ref[...].astype(BF16)
    gn = gn_ref[...]
    for r in range(q.shape[0] // rb):
        rs = slice(r * rb, (r + 1) * rb)
        qr = q[rs]
        p = (_dot_nt(qr, k) * d_ref[rs, :]).astype(BF16)
        o = _dot(p, v) + qf_ref[rs, :] * _dot(qr, s0f) + qb_ref[rs, :] * _dot(qr, s0b)
        o_ref[rs, :] = _gated_group_norm(o, g_ref[rs, :], gn).astype(o_ref.dtype)


def _softmax_t(scores, sink_row):
    m = sink_row
    for s in scores:
        m = jnp.maximum(m, jnp.max(s, axis=0, keepdims=True))
    ps = [jnp.exp(s - m) for s in scores]
    denom = jnp.exp(sink_row - m)
    for p in ps:
        denom = denom + jnp.sum(p, axis=0, keepdims=True)
    return [p.astype(BF16) for p in ps], denom


def _sink_row(sink_ref, kh, queries):
    return jnp.concatenate(
        [jnp.full((1, queries), sink_ref[kh * ATT_GROUPS + g], F32) for g in range(ATT_GROUPS)], axis=1)


def _att_ctx_body(hd, n_ctx, sink_ref, q_ref, k_ref, v_ref, o_ref, nk_ref, nv_ref):
    rows = q_ref.shape[0]
    scale = hd ** -0.5

    def compute():
        for kh in range(ATT_KV_HEADS):
            cs = slice(kh * hd, (kh + 1) * hd)
            heads = [kh * ATT_GROUPS + g for g in range(ATT_GROUPS)]
            qg = jnp.concatenate([q_ref[:, hh * hd:(hh + 1) * hd] for hh in heads], axis=0)
            k32 = k_ref[:, cs]
            v32 = v_ref[:, cs]
            nk_ref[pl.ds(kh, rows, stride=ATT_KV_HEADS), :] = k32
            nv_ref[pl.ds(kh, rows, stride=ATT_KV_HEADS), :] = v32
            k = (k32 * scale).astype(BF16)
            (p,), denom = _softmax_t([_dot_nt(k, qg)], _sink_row(sink_ref, kh, rows))
            o_t = _dot(v32.T.astype(BF16), p) / denom
            for g, hh in enumerate(heads):
                o_ref[:, hh * hd:(hh + 1) * hd] = o_t[:, g * rows:(g + 1) * rows].T.astype(o_ref.dtype)

    _ctx_or_fill(n_ctx, o_ref, compute)


def _att_lat_body(hd, length, sink_ref, buf_ref, q_ref, kp_ref, kc_ref, kn_ref, vp_ref, vc_ref, vn_ref,
                  ck_ref, cv_ref, cos_ref, sa_ref, sb_ref, o_ref):
    del buf_ref
    n = pl.program_id(1)
    nb = length // BLOCK
    scale = hd ** -0.5

    def rope(x, blk):
        rows = pl.ds(pl.multiple_of(blk * BLOCK, BLOCK), BLOCK)
        return (x * cos_ref[rows, :] + pltpu.roll(x, 3 * hd // 4, 1) * sa_ref[rows, :]
                + pltpu.roll(x, hd // 4, 1) * sb_ref[rows, :])

    bp = jnp.maximum(n - 1, 0)
    bn = jnp.minimum(n + 1, nb - 1)
    kpos = (n - 1) * BLOCK + lax.broadcasted_iota(jnp.int32, (3 * BLOCK, BLOCK), 0)
    qpos = n * BLOCK + lax.broadcasted_iota(jnp.int32, (3 * BLOCK, BLOCK), 1)
    mask = (jnp.abs(kpos - qpos) <= WINDOW) & (kpos >= 0) & (kpos < length)
    mask = jnp.concatenate([mask] * ATT_GROUPS, axis=1)
    for kh in range(ATT_KV_HEADS):
        cs = slice(kh * hd, (kh + 1) * hd)
        heads = [kh * ATT_GROUPS + g for g in range(ATT_GROUPS)]
        kw = (jnp.concatenate(
            [rope(kp_ref[:, cs], bp), rope(kc_ref[:, cs], n), rope(kn_ref[:, cs], bn)], axis=0)
              * scale).astype(BF16)
        vw_t = jnp.concatenate([vp_ref[:, cs], vc_ref[:, cs], vn_ref[:, cs]], axis=0).T.astype(BF16)
        qg = jnp.concatenate(
            [rope(q_ref[:, hh * hd:(hh + 1) * hd].astype(F32), n) for hh in heads], axis=0).astype(BF16)
        past = ck_ref.shape[0] // ATT_KV_HEADS
        ck = ck_ref[pl.ds(kh, past, stride=ATT_KV_HEADS), :]
        cv = cv_ref[pl.ds(kh, past, stride=ATT_KV_HEADS), :]
        sw = jnp.where(mask, _dot_nt(kw, qg), NEG_INF)
        sc = _dot_nt((ck * scale).astype(BF16), qg)
        (pw, pc), denom = _softmax_t([sw, sc], _sink_row(sink_ref, kh, BLOCK))
        o_t = (_dot(vw_t, pw) + _dot(cv.T.astype(BF16), pc)) / denom
        for g, hh in enumerate(heads):
            o_ref[:, hh * hd:(hh + 1) * hd] = o_t[:, g * BLOCK:(g + 1) * BLOCK].T.astype(o_ref.dtype)


def _rope_tables(length, hd):
    half = hd // 2
    quarter = half // 2
    rows = length // GRID_W
    row = jnp.repeat(jnp.arange(rows), GRID_W).astype(F32)
    col = jnp.tile(jnp.arange(GRID_W), rows).astype(F32)
    inv_freq = ROPE_BASE ** (-jnp.arange(quarter, dtype=F32) / quarter)
    ang_r = row[:, None] * inv_freq[None, :]
    ang_c = col[:, None] * inv_freq[None, :]
    zeros = jnp.zeros_like(ang_r)
    cos = jnp.concatenate([jnp.cos(ang_r)] * 2 + [jnp.cos(ang_c)] * 2, axis=-1)
    sin_hi = jnp.concatenate([-jnp.sin(ang_r), zeros, -jnp.sin(ang_c), zeros], axis=-1)
    sin_lo = jnp.concatenate([zeros, jnp.sin(ang_r), zeros, jnp.sin(ang_c)], axis=-1)
    return cos, sin_hi, sin_lo


def kernel(x_prompt, x_sample, cache_att_k, cache_att_v, state_ret_fwd, state_ret_bwd, c, c_ctx, w_mod, b_mod, norm_mix_pre, norm_mix_post, norm_ffn_pre, norm_ffn_post, w_in, ret_log_decay_fwd, ret_log_decay_bwd, ret_gn_w, att_sink, w_branch_ret, w_branch_att, w_out, w_ffn_up, w_ffn_down):
    nbc, lc, d = x_prompt.shape
    nbl, ll, _ = x_sample.shape
    depth = w_mod.shape[0]
    assert depth == 1
    tc = nbc * lc
    tl = nbl * ll
    t = tc + tl
    ret_w = w_branch_ret.shape[1]
    att_w = w_branch_att.shape[1]
    dk = ret_w // RET_HEADS
    hd = att_w // ATT_HEADS
    kv_w = ATT_KV_HEADS * hd
    hidden = w_ffn_down.shape[1]
    past = cache_att_k.shape[2]
    assert ret_w == att_w and ll % GRID_W == 0

    xp = x_prompt.reshape(tc, d)
    xs = x_sample.reshape(tl, d)
    w_in2 = w_in.reshape(d, -1)

    groups = 8
    assert 1 + nbl <= groups
    cond = jnp.concatenate([c_ctx[None, :], c, jnp.zeros((groups - 1 - nbl, d), F32)], axis=0)
    mods = _adaln(cond, w_mod.reshape(d, 6 * d), b_mod.reshape(1, 6 * d)).reshape(groups, 6, 1, d)

    def group_of(tr):
        return lambda i: jnp.where(i * tr < tc, 0, 1 + (i * tr - tc) // ll)

    def vec(a):
        return a.reshape(1, d)

    def row_spec(tr, width, off=0):
        return pl.BlockSpec((tr, width), lambda i: (i + off, 0))

    def x_specs(tr):
        nct = tc // tr
        return [pl.BlockSpec((tr, d), lambda i: (jnp.minimum(i, nct - 1), 0)),
                pl.BlockSpec((tr, d), lambda i: (jnp.maximum(i - nct, 0), 0))]

    vec_spec = pl.BlockSpec((1, d), lambda i: (0, 0))

    tr = 256
    g = group_of(tr)
    h = pl.pallas_call(
        functools.partial(_prenorm_body, tc // tr),
        out_shape=jax.ShapeDtypeStruct((t, d), BF16),
        grid=(t // tr,),
        in_specs=x_specs(tr) + [vec_spec, _mods_spec(d, 0, g), _mods_spec(d, 1, g)],
        out_specs=row_spec(tr, d),
        compiler_params=_params(1),
        name="prenorm",
    )(xp, xs, vec(norm_mix_pre), mods, mods)

    tm, tn = 1024, 512
    tm_big = _row_tile(t, (1536, 1024))
    kv_off = 4 * ret_w + att_w
    gate_off = kv_off + 2 * kv_w
    n_front = kv_off // tn
    w_up = w_ffn_up.reshape(d, 2 * hidden)
    w_down = w_ffn_down.reshape(hidden, d)
    proj, w_up16 = _matmul(
        h, w_in2, n_out=kv_off + 2 * d,
        w_col_block=lambda j: jnp.where(j < n_front, j, j + (gate_off - kv_off) // tn),
        out_dtype=BF16, tm=tm_big, tn=tn, name="in_proj", cast_along=(w_up,))
    kv = _matmul(h, w_in2, n_out=2 * kv_w, w_col_block=lambda j: j + kv_off // tn,
                 out_dtype=F32, tm=tm_big, tn=tn, name="in_proj_kv")

    d_tab, qf_tab, qb_tab, kf_tab, kb_tab = _decay_tables(
        ret_log_decay_fwd.reshape(RET_HEADS), ret_log_decay_bwd.reshape(RET_HEADS), ll, lc, dk)
    gn = ret_gn_w.reshape(1, ret_w)
    ctx_steps = t // lc

    def cb(b):
        return jnp.minimum(b, nbc - 1)

    state_spec = pl.BlockSpec((None, None, RET_HEADS, dk, dk), lambda b: (cb(b), 0, 0, 0, 0))
    ret_ctx, s_f, s_b = pl.pallas_call(
        functools.partial(_ret_ctx_body, dk, nbc),
        out_shape=(jax.ShapeDtypeStruct((t, ret_w), BF16),
                   jax.ShapeDtypeStruct((nbc, 1, RET_HEADS, dk, dk), F32),
                   jax.ShapeDtypeStruct((nbc, 1, RET_HEADS, dk, dk), F32)),
        grid=(ctx_steps,),
        in_specs=[pl.BlockSpec((lc, ret_w), lambda b: (cb(b), 0)),
                  pl.BlockSpec((lc, ret_w), lambda b: (cb(b), 1)),
                  pl.BlockSpec((lc, ret_w), lambda b: (cb(b), 2)),
                  pl.BlockSpec((lc, ret_w), lambda b: (cb(b), 3)),
                  pl.BlockSpec((RET_HEADS, lc, lc), lambda b: (0, 0, 0)),
                  pl.BlockSpec((RET_HEADS, 8, lc), lambda b: (0, 0, 0)),
                  pl.BlockSpec((RET_HEADS, 8, lc), lambda b: (0, 0, 0)),
                  pl.BlockSpec((1, ret_w), lambda b: (0, 0))],
        out_specs=(pl.BlockSpec((lc, ret_w), lambda b: (b, 0)), state_spec, state_spec),
        compiler_params=_params(1),
        name="ret_ctx",
    )(proj, proj, proj, proj, d_tab, kf_tab, kb_tab, gn)

    cos_r, sin_hi_r, sin_lo_r = _rope_tables(ll, dk)
    rb0 = tc // ll
    hb = ret_w // dk
    ret_out = pl.pallas_call(
        functools.partial(_ret_lat_body, dk, 256),
        out_shape=jax.ShapeDtypeStruct((t, ret_w), BF16),
        grid=(RET_HEADS, nbl),
        in_specs=[pl.BlockSpec(memory_space=pl.ANY),
                  pl.BlockSpec((ll, dk), lambda hh, b: (rb0 + b, hh)),
                  pl.BlockSpec((ll, dk), lambda hh, b: (rb0 + b, hb + hh)),
                  pl.BlockSpec((ll, dk), lambda hh, b: (rb0 + b, 2 * hb + hh)),
                  pl.BlockSpec((ll, dk), lambda hh, b: (rb0 + b, 3 * hb + hh)),
                  pl.BlockSpec((None, ll, ll), lambda hh, b: (hh, 0, 0)),
                  pl.BlockSpec((None, ll, dk), lambda hh, b: (hh, 0, 0)),
                  pl.BlockSpec((None, ll, dk), lambda hh, b: (hh, 0, 0)),
                  pl.BlockSpec((ll, dk), lambda hh, b: (0, 0)),
                  pl.BlockSpec((ll, dk), lambda hh, b: (0, 0)),
                  pl.BlockSpec((None, None, None, dk, dk), lambda hh, b: (b, 0, hh, 0, 0)),
                  pl.BlockSpec((None, None, None, dk, dk), lambda hh, b: (b, 0, hh, 0, 0)),
                  pl.BlockSpec((1, dk), lambda hh, b: (0, hh))],
        out_specs=pl.BlockSpec((ll, dk), lambda hh, b: (rb0 + b, hh)),
        input_output_aliases={0: 0},
        compiler_params=_params(2),
        name="ret_lat",
    )(ret_ctx, proj, proj, proj, proj, d_tab, qf_tab, qb_tab, cos_r, sin_hi_r + sin_lo_r,
      state_ret_fwd, state_ret_bwd, gn)

    sink = att_sink.reshape(ATT_HEADS)
    aq_blk = 4 * ret_w // att_w
    new_kv_spec = pl.BlockSpec((lc * ATT_KV_HEADS, hd), lambda b: (cb(b), 0))
    att_ctx, new_k, new_v = pl.pallas_call(
        functools.partial(_att_ctx_body, hd, nbc),
        out_shape=(jax.ShapeDtypeStruct((t, att_w), BF16),
                   jax.ShapeDtypeStruct((tc * ATT_KV_HEADS, hd), F32),
                   jax.ShapeDtypeStruct((tc * ATT_KV_HEADS, hd), F32)),
        grid=(ctx_steps,),
        in_specs=[pl.BlockSpec(memory_space=pltpu.SMEM),
                  pl.BlockSpec((lc, att_w), lambda b: (cb(b), aq_blk)),
                  pl.BlockSpec((lc, kv_w), lambda b: (cb(b), 0)),
                  pl.BlockSpec((lc, kv_w), lambda b: (cb(b), 1))],
        out_specs=(pl.BlockSpec((lc, att_w), lambda b: (b, 0)), new_kv_spec, new_kv_spec),
        compiler_params=_params(1),
        name="att_ctx",
    )(sink, proj, kv, kv)
    new_k = new_k.reshape(nbc, 1, lc, ATT_KV_HEADS, hd)
    new_v = new_v.reshape(nbc, 1, lc, ATT_KV_HEADS, hd)

    cos_a, sin_hi_a, sin_lo_a = _rope_tables(ll, hd)
    nqb = ll // BLOCK
    qb0 = tc // BLOCK

    def kv_spec(shift, col):
        return pl.BlockSpec(
            (BLOCK, kv_w), lambda b, n: (qb0 + b * nqb + jnp.clip(n + shift, 0, nqb - 1), col))

    tab_spec = pl.BlockSpec((ll, hd), lambda b, n: (0, 0))
    att_out = pl.pallas_call(
        functools.partial(_att_lat_body, hd, ll),
        out_shape=jax.ShapeDtypeStruct((t, att_w), BF16),
        grid=(nbl, nqb),
        in_specs=[pl.BlockSpec(memory_space=pltpu.SMEM),
                  pl.BlockSpec(memory_space=pl.ANY),
                  pl.BlockSpec((BLOCK, att_w), lambda b, n: (qb0 + b * nqb + n, aq_blk)),
                  kv_spec(-1, 0), kv_spec(0, 0), kv_spec(1, 0),
                  kv_spec(-1, 1), kv_spec(0, 1), kv_spec(1, 1),
                  pl.BlockSpec((past * ATT_KV_HEADS, hd), lambda b, n: (b, 0)),
                  pl.BlockSpec((past * ATT_KV_HEADS, hd), lambda b, n: (b, 0)),
                  tab_spec, tab_spec, tab_spec],
        out_specs=pl.BlockSpec((BLOCK, att_w), lambda b, n: (qb0 + b * nqb + n, 0)),
        input_output_aliases={1: 0},
        compiler_params=_params(2),
        name="att_lat",
    )(sink, att_ctx, proj, kv, kv, kv, kv, kv, kv,
      cache_att_k.reshape(nbl * past * ATT_KV_HEADS, hd), cache_att_v.reshape(nbl * past * ATT_KV_HEADS, hd),
      cos_a, sin_hi_a, sin_lo_a)

    gr_blk = kv_off // tn
    mix_pre = pl.pallas_call(
        _merge_body,
        out_shape=jax.ShapeDtypeStruct((t, d), BF16),
        grid=(d // tn, t // tm),
        in_specs=[pl.BlockSpec((tm, ret_w), lambda j, i: (i, 0)),
                  pl.BlockSpec((tm, att_w), lambda j, i: (i, 0)),
                  pl.BlockSpec((ret_w, tn), lambda j, i: (0, j)),
                  pl.BlockSpec((att_w, tn), lambda j, i: (0, j)),
                  pl.BlockSpec((tm, tn), lambda j, i: (i, gr_blk + j)),
                  pl.BlockSpec((tm, tn), lambda j, i: (i, gr_blk + d // tn + j))],
        out_specs=pl.BlockSpec((tm, tn), lambda j, i: (i, j)),
        compiler_params=_params(2),
        name="merge",
    )(ret_out, att_out, w_branch_ret.reshape(ret_w, d), w_branch_att.reshape(att_w, d), proj, proj)

    mix = _matmul(mix_pre, w_out.reshape(d, d), n_out=d, w_col_block=lambda j: j,
                  out_dtype=BF16, tm=tm_big, tn=tn, name="out_proj")

    tr = 256
    g = group_of(tr)
    h2 = pl.pallas_call(
        functools.partial(_post1_body, tc // tr),
        out_shape=jax.ShapeDtypeStruct((t, d), BF16),
        grid=(t // tr,),
        in_specs=x_specs(tr) + [row_spec(tr, d), _mods_spec(d, 2, g), vec_spec, vec_spec,
                                _mods_spec(d, 3, g), _mods_spec(d, 4, g)],
        out_specs=row_spec(tr, d),
        compiler_params=_params(1),
        name="post_mix",
    )(xp, xs, mix, mods, vec(norm_mix_post), vec(norm_ffn_pre), mods, mods)

    tnh = 256
    tm_ffn = _row_tile(t, (2048, 1024))
    n_j = hidden // tnh
    n_i = t // tm_ffn
    wd_spec = _cast_along_spec(w_down, n_i, n_j * n_i)
    act, w_down16 = pl.pallas_call(
        _ffn_up_body,
        out_shape=(jax.ShapeDtypeStruct((t, hidden), BF16), jax.ShapeDtypeStruct((hidden, d), BF16)),
        grid=(n_j, n_i),
        in_specs=[pl.BlockSpec((tm_ffn, d), lambda j, i: (i, 0)),
                  pl.BlockSpec((d, tnh), lambda j, i: (0, j)),
                  pl.BlockSpec((d, tnh), lambda j, i: (0, n_j + j)),
                  wd_spec],
        out_specs=(pl.BlockSpec((tm_ffn, tnh), lambda j, i: (i, j)), wd_spec),
        compiler_params=_params(2),
        name="ffn_up",
    )(h2, w_up16, w_up16, w_down)
    f = _matmul(act, w_down16, n_out=d, w_col_block=lambda j: j, out_dtype=BF16, tm=512, tn=tn,
                name="ffn_down")

    def final(x, off_tiles, name):
        rows = x.shape[0]
        gg = group_of(tr)

        def mod_spec(m):
            return pl.BlockSpec((None, None, 1, d), lambda i: (gg(i + off_tiles), m, 0, 0))

        return pl.pallas_call(
            _final_body,
            out_shape=jax.ShapeDtypeStruct((rows, d), F32),
            grid=(rows // tr,),
            in_specs=[row_spec(tr, d), row_spec(tr, d, off_tiles), row_spec(tr, d, off_tiles),
                      mod_spec(2), vec_spec, mod_spec(5), vec_spec],
            out_specs=row_spec(tr, d),
            compiler_params=_params(1),
            name=name,
        )(x, mix, f, mods, vec(norm_mix_post), mods, vec(norm_ffn_post))

    y_prompt = final(xp, 0, "final_ctx").reshape(nbc, lc, d)
    y_sample = final(xs, tc // tr, "final_lat").reshape(nbl, ll, d)

    return y_prompt, y_sample, new_k, new_v, s_f, s_b
```

```python
import functools

import jax
import jax.numpy as jnp
from jax import lax
from jax.experimental import pallas as pl
from jax.experimental.pallas import tpu as pltpu

F32 = jnp.float32
BF16 = jnp.bfloat16

GRID_W = 64
RET_HEADS = 8
ATT_HEADS = 16
ATT_KV_HEADS = 4
ATT_GROUPS = ATT_HEADS // ATT_KV_HEADS
WINDOW = 128
BLOCK = 128
ROPE_BASE = 10000.0
NORM_EPS = 1e-6
NEG_INF = -1e30

VMEM_LIMIT_BYTES = 56 * 1024 * 1024
LANES = 128
FFN_ROW_CHUNK = 512
MERGE_ROW_CHUNK = 512


def _params(n_axes):
    return pltpu.CompilerParams(
        dimension_semantics=("arbitrary",) * n_axes, vmem_limit_bytes=VMEM_LIMIT_BYTES)


def _row_tile(rows, candidates):
    return next(c for c in candidates if rows % c == 0)


def _rms_scale(x):
    return lax.rsqrt(jnp.mean(x * x, axis=-1, keepdims=True) + NORM_EPS)


def _dot(a, b):
    return jnp.dot(a, b, preferred_element_type=F32)


def _dot_nt(a, b):
    return lax.dot_general(a, b, (((1,), (1,)), ((), ())), preferred_element_type=F32)


def _adaln_body(c_ref, w_ref, b_ref, o_ref):
    o_ref[...] = _dot(jax.nn.silu(c_ref[...]), w_ref[...]) + b_ref[...]


def _adaln(cond, w_mod, b_mod, tn=512):
    rows, d = cond.shape
    n = w_mod.shape[1]
    return pl.pallas_call(
        _adaln_body,
        out_shape=jax.ShapeDtypeStruct((rows, n), F32),
        grid=(n // tn,),
        in_specs=[pl.BlockSpec((rows, d), lambda j: (0, 0)),
                  pl.BlockSpec((d, tn), lambda j: (0, j)),
                  pl.BlockSpec((1, tn), lambda j: (0, j))],
        out_specs=pl.BlockSpec((rows, tn), lambda j: (0, j)),
        compiler_params=_params(1),
        name="adaln",
    )(cond, w_mod, b_mod)


def _mods_spec(d, m, group_of):
    return pl.BlockSpec((None, None, 1, d), lambda i: (group_of(i), m, 0, 0))


def _prenorm_body(nc_tiles, xp_ref, xs_ref, w_ref, sh_ref, sc_ref, h_ref):
    def run(x_ref):
        x = x_ref[...]
        y = x * _rms_scale(x) * w_ref[...]
        h_ref[...] = (y * (1.0 + sc_ref[...]) + sh_ref[...]).astype(h_ref.dtype)

    i = pl.program_id(0)
    pl.when(i < nc_tiles)(lambda: run(xp_ref))
    pl.when(i >= nc_tiles)(lambda: run(xs_ref))


def _residual1(x_ref, mix_ref, g1_ref, wpost_ref):
    m = mix_ref[...].astype(F32)
    return x_ref[...] + g1_ref[...] * (m * _rms_scale(m) * wpost_ref[...])


def _post1_body(nc_tiles, xp_ref, xs_ref, mix_ref, g1_ref, wpost_ref, wpre_ref, sh_ref, sc_ref, h2_ref):
    def run(x_ref):
        x1 = _residual1(x_ref, mix_ref, g1_ref, wpost_ref)
        y = x1 * _rms_scale(x1) * wpre_ref[...]
        h2_ref[...] = (y * (1.0 + sc_ref[...]) + sh_ref[...]).astype(h2_ref.dtype)

    i = pl.program_id(0)
    pl.when(i < nc_tiles)(lambda: run(xp_ref))
    pl.when(i >= nc_tiles)(lambda: run(xs_ref))


def _final_body(x_ref, mix_ref, f_ref, g1_ref, wpost_ref, g2_ref, w_ref, y_ref):
    x1 = _residual1(x_ref, mix_ref, g1_ref, wpost_ref)
    f = f_ref[...].astype(F32)
    y_ref[...] = x1 + g2_ref[...] * (f * _rms_scale(f) * w_ref[...])


def _mm_body(n_cast, x_ref, w_ref, *refs):
    o_ref = refs[n_cast]
    o_ref[...] = _dot(x_ref[...], w_ref[...].astype(BF16)).astype(o_ref.dtype)
    for src_ref, dst_ref in zip(refs[:n_cast], refs[n_cast + 1:]):
        _cast_along(src_ref, dst_ref)


BF16_SUBLANES = 16


def _cast_along_spec(src, steps, linear_step):
    units = src.shape[0] // BF16_SUBLANES
    n_blocks = max(b for b in range(1, min(units, steps) + 1) if units % b == 0)
    return pl.BlockSpec((src.shape[0] // n_blocks, src.shape[1]),
                        lambda *idx: (jnp.minimum(linear_step(*idx), n_blocks - 1), 0))


def _cast_along(src_ref, dst_ref):
    dst_ref[...] = src_ref[...].astype(dst_ref.dtype)


def _matmul(x, w, *, n_out, w_col_block, out_dtype, tm, tn, name, cast_along=()):
    m, k = x.shape
    n_i = m // tm
    steps = (n_out // tn) * n_i
    cast_specs = [_cast_along_spec(src, steps, lambda j, i: j * n_i + i) for src in cast_along]
    cast_shapes = [jax.ShapeDtypeStruct(src.shape, BF16) for src in cast_along]
    outs = pl.pallas_call(
        functools.partial(_mm_body, len(cast_along)),
        out_shape=[jax.ShapeDtypeStruct((m, n_out), out_dtype)] + cast_shapes,
        grid=(n_out // tn, n_i),
        in_specs=[pl.BlockSpec((tm, k), lambda j, i: (i, 0)),
                  pl.BlockSpec((k, tn), lambda j, i: (0, w_col_block(j)))] + cast_specs,
        out_specs=[pl.BlockSpec((tm, tn), lambda j, i: (i, j))] + cast_specs,
        compiler_params=_params(2),
        name=name,
    )(x, w, *cast_along)
    return outs if cast_along else outs[0]


def _merge_body(r_ref, a_ref, wr_ref, wa_ref, gr_ref, ga_ref, wo_ref, o_ref, wo16_ref):
    _cast_along(wo_ref, wo16_ref)
    wr = wr_ref[...]
    wa = wa_ref[...]
    for c in range(o_ref.shape[0] // MERGE_ROW_CHUNK):
        rs = slice(c * MERGE_ROW_CHUNK, (c + 1) * MERGE_ROW_CHUNK)
        gr = jax.nn.sigmoid(gr_ref[rs, :].astype(F32))
        ga = jax.nn.sigmoid(ga_ref[rs, :].astype(F32))
        o_ref[rs, :] = (gr * _dot(r_ref[rs, :], wr) + ga * _dot(a_ref[rs, :], wa)).astype(o_ref.dtype)


def _ffn_up_body(h_ref, wa_ref, wb_ref, wd_ref, o_ref, wd16_ref):
    wa = wa_ref[...]
    wb = wb_ref[...]
    for c in range(h_ref.shape[0] // FFN_ROW_CHUNK):
        rs = slice(c * FFN_ROW_CHUNK, (c + 1) * FFN_ROW_CHUNK)
        h = h_ref[rs, :]
        o_ref[rs, :] = (jax.nn.silu(_dot(h, wa)) * _dot(h, wb)).astype(o_ref.dtype)
    _cast_along(wd_ref, wd16_ref)


def _decay_body(lc, ldf_ref, ldb_ref, d_ref, qf_ref, qb_ref, kf_ref, kb_ref):
    h = pl.program_id(0)
    r = pl.program_id(1)
    rb, l = d_ref.shape

    def log_gamma(ld_ref, shape):
        return -jnp.exp(jnp.full(shape, ld_ref[h], F32))

    def rows(shape):
        return (lax.broadcasted_iota(jnp.int32, shape, 0) + r * rb).astype(F32)

    def cols(shape):
        return lax.broadcasted_iota(jnp.int32, shape, 1).astype(F32)

    diff = rows((rb, l)) - cols((rb, l))
    d_f = jnp.where(diff >= 0, jnp.exp(log_gamma(ldf_ref, (rb, l)) * jnp.maximum(diff, 0.0)), 0.0)
    d_b = jnp.where(diff <= 0, jnp.exp(log_gamma(ldb_ref, (rb, l)) * jnp.maximum(-diff, 0.0)), 0.0)
    d_ref[...] = d_f + d_b
    qs = qf_ref.shape
    qf_ref[...] = jnp.exp(log_gamma(ldf_ref, qs) * (rows(qs) + 1.0))
    qb_ref[...] = jnp.exp(log_gamma(ldb_ref, qs) * (float(l) - rows(qs)))
    ks = kf_ref.shape
    kf_ref[...] = jnp.exp(log_gamma(ldf_ref, ks) * (float(lc - 1) - cols(ks)))
    kb_ref[...] = jnp.exp(log_gamma(ldb_ref, ks) * cols(ks))


def _decay_tables(ldf, ldb, l, lc, dk):
    rb = 256
    return pl.pallas_call(
        functools.partial(_decay_body, lc),
        out_shape=(jax.ShapeDtypeStruct((RET_HEADS, l, l), F32),
                   jax.ShapeDtypeStruct((RET_HEADS, l, dk), F32),
                   jax.ShapeDtypeStruct((RET_HEADS, l, dk), F32),
                   jax.ShapeDtypeStruct((RET_HEADS, 8, lc), F32),
                   jax.ShapeDtypeStruct((RET_HEADS, 8, lc), F32)),
        grid=(RET_HEADS, l // rb),
        in_specs=[pl.BlockSpec(memory_space=pltpu.SMEM), pl.BlockSpec(memory_space=pltpu.SMEM)],
        out_specs=(pl.BlockSpec((None, rb, l), lambda h, r: (h, r, 0)),
                   pl.BlockSpec((None, rb, dk), lambda h, r: (h, r, 0)),
                   pl.BlockSpec((None, rb, dk), lambda h, r: (h, r, 0)),
                   pl.BlockSpec((None, 8, lc), lambda h, r: (h, 0, 0)),
                   pl.BlockSpec((None, 8, lc), lambda h, r: (h, 0, 0))),
        compiler_params=_params(2),
        name="decay_tables",
    )(ldf, ldb)


def _gated_group_norm(o, g_bf16, gn):
    on = o * _rms_scale(o) * gn
    return jax.nn.silu(g_bf16.astype(F32)) * on


def _ctx_or_fill(n_ctx, o_ref, compute):
    b = pl.program_id(0)
    pl.when(b < n_ctx)(compute)

    @pl.when(b >= n_ctx)
    def _():
        o_ref[...] = jnp.zeros(o_ref.shape, o_ref.dtype)


def _ret_ctx_body(dk, n_ctx, q_ref, k_ref, v_ref, g_ref, d_ref, kf_ref, kb_ref, gn_ref, o_ref, sf_ref, sb_ref):
    def compute():
        for h in range(RET_HEADS):
            sl = slice(h * dk, (h + 1) * dk)
            q = q_ref[:, sl]
            k32 = k_ref[:, sl].astype(F32) * (dk ** -0.5)
            v = v_ref[:, sl]
            p = (_dot_nt(q, k32.astype(BF16)) * d_ref[h]).astype(BF16)
            o = _dot(p, v)
            kt = k32.T
            sf_ref[h] = _dot((kt * kf_ref[h][0:1, :]).astype(BF16), v)
            sb_ref[h] = _dot((kt * kb_ref[h][0:1, :]).astype(BF16), v)
            o_ref[:, sl] = _gated_group_norm(o, g_ref[:, sl], gn_ref[:, sl]).astype(o_ref.dtype)

    _ctx_or_fill(n_ctx, o_ref, compute)


def _ret_lat_body(dk, rb, buf_ref, q_ref, k_ref, v_ref, g_ref, d_ref, qf_ref, qb_ref, cos_ref, sin_ref,
                  s0f_ref, s0b_ref, gn_ref, o_ref):
    del buf_ref
    cos = cos_ref[...]
    sin = sin_ref[...]

    def rope(x):
        swapped = jnp.concatenate(
            [pltpu.roll(x[:, c * LANES:(c + 1) * LANES], LANES // 2, 1) for c in range(dk // LANES)], axis=1)
        return x * cos + swapped * sin

    q = rope(q_ref[...].astype(F32)).astype(BF16)
    k = rope(k_ref[...].astype(F32) * (dk ** -0.5)).astype(BF16)
    v = v_ref[...]
    s0f = s0f_ref[...].astype(BF16)
    s0b = s0b_ref[...].astype(BF16)
    gn = gn_ref[...]
    for r in range(q.shape[0] // rb):
        rs = slice(r * rb, (r + 1) * rb)
        qr = q[rs]
        p = (_dot_nt(qr, k) * d_ref[rs, :]).astype(BF16)
        o = _dot(p, v) + qf_ref[rs, :] * _dot(qr, s0f) + qb_ref[rs, :] * _dot(qr, s0b)
        o_ref[rs, :] = _gated_group_norm(o, g_ref[rs, :], gn).astype(o_ref.dtype)


def _softmax_t(scores, sink_row):
    m = sink_row
    for s in scores:
        m = jnp.maximum(m, jnp.max(s, axis=0, keepdims=True))
    ps = [jnp.exp(s - m) for s in scores]
    denom = jnp.exp(sink_row - m)
    for p in ps:
        denom = denom + jnp.sum(p, axis=0, keepdims=True)
    return [p.astype(BF16) for p in ps], denom


def _sink_row(sink_ref, kh, queries):
    return jnp.concatenate(
        [jnp.full((1, queries), sink_ref[kh * ATT_GROUPS + g], F32) for g in range(ATT_GROUPS)], axis=1)


def _att_ctx_body(hd, n_ctx, sink_ref, q_ref, k_ref, v_ref, wr_ref, wa_ref, o_ref, nk_ref, nv_ref,
                  wr16_ref, wa16_ref):
    rows = q_ref.shape[0]
    scale = hd ** -0.5
    _cast_along(wr_ref, wr16_ref)
    _cast_along(wa_ref, wa16_ref)

    def compute():
        for kh in range(ATT_KV_HEADS):
            cs = slice(kh * hd, (kh + 1) * hd)
            heads = [kh * ATT_GROUPS + g for g in range(ATT_GROUPS)]
            qg = jnp.concatenate([q_ref[:, hh * hd:(hh + 1) * hd] for hh in heads], axis=0)
            k32 = k_ref[:, cs]
            v32 = v_ref[:, cs]
            nk_ref[pl.ds(kh, rows, stride=ATT_KV_HEADS), :] = k32
            nv_ref[pl.ds(kh, rows, stride=ATT_KV_HEADS), :] = v32
            k = (k32 * scale).astype(BF16)
            (p,), denom = _softmax_t([_dot_nt(k, qg)], _sink_row(sink_ref, kh, rows))
            o_t = _dot(v32.T.astype(BF16), p) / denom
            for g, hh in enumerate(heads):
                o_ref[:, hh * hd:(hh + 1) * hd] = o_t[:, g * rows:(g + 1) * rows].T.astype(o_ref.dtype)

    _ctx_or_fill(n_ctx, o_ref, compute)


def _att_lat_body(hd, length, sink_ref, buf_ref, q_ref, kp_ref, kc_ref, kn_ref, vp_ref, vc_ref, vn_ref,
                  ck_ref, cv_ref, cos_ref, sa_ref, sb_ref, o_ref):
    del buf_ref
    n = pl.program_id(1)
    nb = length // BLOCK
    scale = hd ** -0.5

    def rope(x, blk):
        rows = pl.ds(pl.multiple_of(blk * BLOCK, BLOCK), BLOCK)
        return (x * cos_ref[rows, :] + pltpu.roll(x, 3 * hd // 4, 1) * sa_ref[rows, :]
                + pltpu.roll(x, hd // 4, 1) * sb_ref[rows, :])

    bp = jnp.maximum(n - 1, 0)
    bn = jnp.minimum(n + 1, nb - 1)
    kpos = (n - 1) * BLOCK + lax.broadcasted_iota(jnp.int32, (3 * BLOCK, BLOCK), 0)
    qpos = n * BLOCK + lax.broadcasted_iota(jnp.int32, (3 * BLOCK, BLOCK), 1)
    mask = (jnp.abs(kpos - qpos) <= WINDOW) & (kpos >= 0) & (kpos < length)
    mask = jnp.concatenate([mask] * ATT_GROUPS, axis=1)
    for kh in range(ATT_KV_HEADS):
        cs = slice(kh * hd, (kh + 1) * hd)
        heads = [kh * ATT_GROUPS + g for g in range(ATT_GROUPS)]
        kw = (jnp.concatenate(
            [rope(kp_ref[:, cs], bp), rope(kc_ref[:, cs], n), rope(kn_ref[:, cs], bn)], axis=0)
              * scale).astype(BF16)
        vw_t = jnp.concatenate([vp_ref[:, cs], vc_ref[:, cs], vn_ref[:, cs]], axis=0).T.astype(BF16)
        qg = jnp.concatenate(
            [rope(q_ref[:, hh * hd:(hh + 1) * hd].astype(F32), n) for hh in heads], axis=0).astype(BF16)
        past = ck_ref.shape[0] // ATT_KV_HEADS
        ck = ck_ref[pl.ds(kh, past, stride=ATT_KV_HEADS), :]
        cv = cv_ref[pl.ds(kh, past, stride=ATT_KV_HEADS), :]
        sw = jnp.where(mask, _dot_nt(kw, qg), NEG_INF)
        sc = _dot_nt((ck * scale).astype(BF16), qg)
        (pw, pc), denom = _softmax_t([sw, sc], _sink_row(sink_ref, kh, BLOCK))
        o_t = (_dot(vw_t, pw) + _dot(cv.T.astype(BF16), pc)) / denom
        for g, hh in enumerate(heads):
            o_ref[:, hh * hd:(hh + 1) * hd] = o_t[:, g * BLOCK:(g + 1) * BLOCK].T.astype(o_ref.dtype)


def _rope_tables(length, hd):
    half = hd // 2
    quarter = half // 2
    rows = length // GRID_W
    row = jnp.repeat(jnp.arange(rows), GRID_W).astype(F32)
    col = jnp.tile(jnp.arange(GRID_W), rows).astype(F32)
    inv_freq = ROPE_BASE ** (-jnp.arange(quarter, dtype=F32) / quarter)
    ang_r = row[:, None] * inv_freq[None, :]
    ang_c = col[:, None] * inv_freq[None, :]
    zeros = jnp.zeros_like(ang_r)
    cos = jnp.concatenate([jnp.cos(ang_r)] * 2 + [jnp.cos(ang_c)] * 2, axis=-1)
    sin_hi = jnp.concatenate([-jnp.sin(ang_r), zeros, -jnp.sin(ang_c), zeros], axis=-1)
    sin_lo = jnp.concatenate([zeros, jnp.sin(ang_r), zeros, jnp.sin(ang_c)], axis=-1)
    return cos, sin_hi, sin_lo


def kernel(x_prompt, x_sample, cache_att_k, cache_att_v, state_ret_fwd, state_ret_bwd, c, c_ctx, w_mod, b_mod, norm_mix_pre, norm_mix_post, norm_ffn_pre, norm_ffn_post, w_in, ret_log_decay_fwd, ret_log_decay_bwd, ret_gn_w, att_sink, w_branch_ret, w_branch_att, w_out, w_ffn_up, w_ffn_down):
    nbc, lc, d = x_prompt.shape
    nbl, ll, _ = x_sample.shape
    depth = w_mod.shape[0]
    assert depth == 1
    tc = nbc * lc
    tl = nbl * ll
    t = tc + tl
    ret_w = w_branch_ret.shape[1]
    att_w = w_branch_att.shape[1]
    dk = ret_w // RET_HEADS
    hd = att_w // ATT_HEADS
    kv_w = ATT_KV_HEADS * hd
    hidden = w_ffn_down.shape[1]
    past = cache_att_k.shape[2]
    assert ret_w == att_w and ll % GRID_W == 0

    xp = x_prompt.reshape(tc, d)
    xs = x_sample.reshape(tl, d)
    w_in2 = w_in.reshape(d, -1)

    groups = 8
    assert 1 + nbl <= groups
    cond = jnp.concatenate([c_ctx[None, :], c, jnp.zeros((groups - 1 - nbl, d), F32)], axis=0)
    mods = _adaln(cond, w_mod.reshape(d, 6 * d), b_mod.reshape(1, 6 * d)).reshape(groups, 6, 1, d)

    def group_of(tr):
        return lambda i: jnp.where(i * tr < tc, 0, 1 + (i * tr - tc) // ll)

    def vec(a):
        return a.reshape(1, d)

    def row_spec(tr, width, off=0):
        return pl.BlockSpec((tr, width), lambda i: (i + off, 0))

    def x_specs(tr):
        nct = tc // tr
        return [pl.BlockSpec((tr, d), lambda i: (jnp.minimum(i, nct - 1), 0)),
                pl.BlockSpec((tr, d), lambda i: (jnp.maximum(i - nct, 0), 0))]

    vec_spec = pl.BlockSpec((1, d), lambda i: (0, 0))

    tr = 256
    g = group_of(tr)
    h = pl.pallas_call(
        functools.partial(_prenorm_body, tc // tr),
        out_shape=jax.ShapeDtypeStruct((t, d), BF16),
        grid=(t // tr,),
        in_specs=x_specs(tr) + [vec_spec, _mods_spec(d, 0, g), _mods_spec(d, 1, g)],
        out_specs=row_spec(tr, d),
        compiler_params=_params(1),
        name="prenorm",
    )(xp, xs, vec(norm_mix_pre), mods, mods)

    tm, tn = 1024, 512
    tm_big = _row_tile(t, (1536, 1024))
    kv_off = 4 * ret_w + att_w
    gate_off = kv_off + 2 * kv_w
    n_front = kv_off // tn
    w_up = w_ffn_up.reshape(d, 2 * hidden)
    w_down = w_ffn_down.reshape(hidden, d)
    proj, w_up16 = _matmul(
        h, w_in2, n_out=kv_off + 2 * d,
        w_col_block=lambda j: jnp.where(j < n_front, j, j + (gate_off - kv_off) // tn),
        out_dtype=BF16, tm=tm_big, tn=tn, name="in_proj", cast_along=(w_up,))
    kv = _matmul(h, w_in2, n_out=2 * kv_w, w_col_block=lambda j: j + kv_off // tn,
                 out_dtype=F32, tm=tm_big, tn=tn, name="in_proj_kv")

    d_tab, qf_tab, qb_tab, kf_tab, kb_tab = _decay_tables(
        ret_log_decay_fwd.reshape(RET_HEADS), ret_log_decay_bwd.reshape(RET_HEADS), ll, lc, dk)
    gn = ret_gn_w.reshape(1, ret_w)
    ctx_steps = t // lc

    def cb(b):
        return jnp.minimum(b, nbc - 1)

    state_spec = pl.BlockSpec((None, None, RET_HEADS, dk, dk), lambda b: (cb(b), 0, 0, 0, 0))
    ret_ctx, s_f, s_b = pl.pallas_call(
        functools.partial(_ret_ctx_body, dk, nbc),
        out_shape=(jax.ShapeDtypeStruct((t, ret_w), BF16),
                   jax.ShapeDtypeStruct((nbc, 1, RET_HEADS, dk, dk), F32),
                   jax.ShapeDtypeStruct((nbc, 1, RET_HEADS, dk, dk), F32)),
        grid=(ctx_steps,),
        in_specs=[pl.BlockSpec((lc, ret_w), lambda b: (cb(b), 0)),
                  pl.BlockSpec((lc, ret_w), lambda b: (cb(b), 1)),
                  pl.BlockSpec((lc, ret_w), lambda b: (cb(b), 2)),
                  pl.BlockSpec((lc, ret_w), lambda b: (cb(b), 3)),
                  pl.BlockSpec((RET_HEADS, lc, lc), lambda b: (0, 0, 0)),
                  pl.BlockSpec((RET_HEADS, 8, lc), lambda b: (0, 0, 0)),
                  pl.BlockSpec((RET_HEADS, 8, lc), lambda b: (0, 0, 0)),
                  pl.BlockSpec((1, ret_w), lambda b: (0, 0))],
        out_specs=(pl.BlockSpec((lc, ret_w), lambda b: (b, 0)), state_spec, state_spec),
        compiler_params=_params(1),
        name="ret_ctx",
    )(proj, proj, proj, proj, d_tab, kf_tab, kb_tab, gn)

    cos_r, sin_hi_r, sin_lo_r = _rope_tables(ll, dk)
    rb0 = tc // ll
    hb = ret_w // dk
    ret_out = pl.pallas_call(
        functools.partial(_ret_lat_body, dk, 256),
        out_shape=jax.ShapeDtypeStruct((t, ret_w), BF16),
        grid=(RET_HEADS, nbl),
        in_specs=[pl.BlockSpec(memory_space=pl.ANY),
                  pl.BlockSpec((ll, dk), lambda hh, b: (rb0 + b, hh)),
                  pl.BlockSpec((ll, dk), lambda hh, b: (rb0 + b, hb + hh)),
                  pl.BlockSpec((ll, dk), lambda hh, b: (rb0 + b, 2 * hb + hh)),
                  pl.BlockSpec((ll, dk), lambda hh, b: (rb0 + b, 3 * hb + hh)),
                  pl.BlockSpec((None, ll, ll), lambda hh, b: (hh, 0, 0)),
                  pl.BlockSpec((None, ll, dk), lambda hh, b: (hh, 0, 0)),
                  pl.BlockSpec((None, ll, dk), lambda hh, b: (hh, 0, 0)),
                  pl.BlockSpec((ll, dk), lambda hh, b: (0, 0)),
                  pl.BlockSpec((ll, dk), lambda hh, b: (0, 0)),
                  pl.BlockSpec((None, None, None, dk, dk), lambda hh, b: (b, 0, hh, 0, 0)),
                  pl.BlockSpec((None, None, None, dk, dk), lambda hh, b: (b, 0, hh, 0, 0)),
                  pl.BlockSpec((1, dk), lambda hh, b: (0, hh))],
        out_specs=pl.BlockSpec((ll, dk), lambda hh, b: (rb0 + b, hh)),
        input_output_aliases={0: 0},
        compiler_params=_params(2),
        name="ret_lat",
    )(ret_ctx, proj, proj, proj, proj, d_tab, qf_tab, qb_tab, cos_r, sin_hi_r + sin_lo_r,
      state_ret_fwd, state_ret_bwd, gn)

    sink = att_sink.reshape(ATT_HEADS)
    aq_blk = 4 * ret_w // att_w
    new_kv_spec = pl.BlockSpec((lc * ATT_KV_HEADS, hd), lambda b: (cb(b), 0))
    w_br = w_branch_ret.reshape(ret_w, d)
    w_ba = w_branch_att.reshape(att_w, d)
    w_br_spec = _cast_along_spec(w_br, ctx_steps, lambda b: b)
    w_ba_spec = _cast_along_spec(w_ba, ctx_steps, lambda b: b)
    att_ctx, new_k, new_v, w_br16, w_ba16 = pl.pallas_call(
        functools.partial(_att_ctx_body, hd, nbc),
        out_shape=(jax.ShapeDtypeStruct((t, att_w), BF16),
                   jax.ShapeDtypeStruct((tc * ATT_KV_HEADS, hd), F32),
                   jax.ShapeDtypeStruct((tc * ATT_KV_HEADS, hd), F32),
                   jax.ShapeDtypeStruct((ret_w, d), BF16),
                   jax.ShapeDtypeStruct((att_w, d), BF16)),
        grid=(ctx_steps,),
        in_specs=[pl.BlockSpec(memory_space=pltpu.SMEM),
                  pl.BlockSpec((lc, att_w), lambda b: (cb(b), aq_blk)),
                  pl.BlockSpec((lc, kv_w), lambda b: (cb(b), 0)),
                  pl.BlockSpec((lc, kv_w), lambda b: (cb(b), 1)),
                  w_br_spec, w_ba_spec],
        out_specs=(pl.BlockSpec((lc, att_w), lambda b: (b, 0)), new_kv_spec, new_kv_spec,
                   w_br_spec, w_ba_spec),
        compiler_params=_params(1),
        name="att_ctx",
    )(sink, proj, kv, kv, w_br, w_ba)
    new_k = new_k.reshape(nbc, 1, lc, ATT_KV_HEADS, hd)
    new_v = new_v.reshape(nbc, 1, lc, ATT_KV_HEADS, hd)

    cos_a, sin_hi_a, sin_lo_a = _rope_tables(ll, hd)
    nqb = ll // BLOCK
    qb0 = tc // BLOCK

    def kv_spec(shift, col):
        return pl.BlockSpec(
            (BLOCK, kv_w), lambda b, n: (qb0 + b * nqb + jnp.clip(n + shift, 0, nqb - 1), col))

    tab_spec = pl.BlockSpec((ll, hd), lambda b, n: (0, 0))
    att_out = pl.pallas_call(
        functools.partial(_att_lat_body, hd, ll),
        out_shape=jax.ShapeDtypeStruct((t, att_w), BF16),
        grid=(nbl, nqb),
        in_specs=[pl.BlockSpec(memory_space=pltpu.SMEM),
                  pl.BlockSpec(memory_space=pl.ANY),
                  pl.BlockSpec((BLOCK, att_w), lambda b, n: (qb0 + b * nqb + n, aq_blk)),
                  kv_spec(-1, 0), kv_spec(0, 0), kv_spec(1, 0),
                  kv_spec(-1, 1), kv_spec(0, 1), kv_spec(1, 1),
                  pl.BlockSpec((past * ATT_KV_HEADS, hd), lambda b, n: (b, 0)),
                  pl.BlockSpec((past * ATT_KV_HEADS, hd), lambda b, n: (b, 0)),
                  tab_spec, tab_spec, tab_spec],
        out_specs=pl.BlockSpec((BLOCK, att_w), lambda b, n: (qb0 + b * nqb + n, 0)),
        input_output_aliases={1: 0},
        compiler_params=_params(2),
        name="att_lat",
    )(sink, att_ctx, proj, kv, kv, kv, kv, kv, kv,
      cache_att_k.reshape(nbl * past * ATT_KV_HEADS, hd), cache_att_v.reshape(nbl * past * ATT_KV_HEADS, hd),
      cos_a, sin_hi_a, sin_lo_a)

    gr_blk = kv_off // tn
    n_im = t // tm_big
    w_out2 = w_out.reshape(d, d)
    w_out_spec = _cast_along_spec(w_out2, (d // tn) * n_im, lambda j, i: j * n_im + i)
    mix_pre, w_out16 = pl.pallas_call(
        _merge_body,
        out_shape=(jax.ShapeDtypeStruct((t, d), BF16), jax.ShapeDtypeStruct((d, d), BF16)),
        grid=(d // tn, n_im),
        in_specs=[pl.BlockSpec((tm_big, ret_w), lambda j, i: (i, 0)),
                  pl.BlockSpec((tm_big, att_w), lambda j, i: (i, 0)),
                  pl.BlockSpec((ret_w, tn), lambda j, i: (0, j)),
                  pl.BlockSpec((att_w, tn), lambda j, i: (0, j)),
                  pl.BlockSpec((tm_big, tn), lambda j, i: (i, gr_blk + j)),
                  pl.BlockSpec((tm_big, tn), lambda j, i: (i, gr_blk + d // tn + j)),
                  w_out_spec],
        out_specs=(pl.BlockSpec((tm_big, tn), lambda j, i: (i, j)), w_out_spec),
        compiler_params=_params(2),
        name="merge",
    )(ret_out, att_out, w_br16, w_ba16, proj, proj, w_out2)

    mix = _matmul(mix_pre, w_out16, n_out=d, w_col_block=lambda j: j,
                  out_dtype=BF16, tm=tm, tn=2 * tn, name="out_proj")

    tr = 256
    g = group_of(tr)
    h2 = pl.pallas_call(
        functools.partial(_post1_body, tc // tr),
        out_shape=jax.ShapeDtypeStruct((t, d), BF16),
        grid=(t // tr,),
        in_specs=x_specs(tr) + [row_spec(tr, d), _mods_spec(d, 2, g), vec_spec, vec_spec,
                                _mods_spec(d, 3, g), _mods_spec(d, 4, g)],
        out_specs=row_spec(tr, d),
        compiler_params=_params(1),
        name="post_mix",
    )(xp, xs, mix, mods, vec(norm_mix_post), vec(norm_ffn_pre), mods, mods)

    tnh = 256
    tm_ffn = _row_tile(t, (2048, 1024))
    n_j = hidden // tnh
    n_i = t // tm_ffn
    wd_spec = _cast_along_spec(w_down, n_j * n_i, lambda j, i: j * n_i + i)
    act, w_down16 = pl.pallas_call(
        _ffn_up_body,
        out_shape=(jax.ShapeDtypeStruct((t, hidden), BF16), jax.ShapeDtypeStruct((hidden, d), BF16)),
        grid=(n_j, n_i),
        in_specs=[pl.BlockSpec((tm_ffn, d), lambda j, i: (i, 0)),
                  pl.BlockSpec((d, tnh), lambda j, i: (0, j)),
                  pl.BlockSpec((d, tnh), lambda j, i: (0, n_j + j)),
                  wd_spec],
        out_specs=(pl.BlockSpec((tm_ffn, tnh), lambda j, i: (i, j)), wd_spec),
        compiler_params=_params(2),
        name="ffn_up",
    )(h2, w_up16, w_up16, w_down)
    f = _matmul(act, w_down16, n_out=d, w_col_block=lambda j: j, out_dtype=BF16, tm=512, tn=tn,
                name="ffn_down")

    def final(x, off_tiles, name):
        rows = x.shape[0]
        gg = group_of(tr)

        def mod_spec(m):
            return pl.BlockSpec((None, None, 1, d), lambda i: (gg(i + off_tiles), m, 0, 0))

        return pl.pallas_call(
            _final_body,
            out_shape=jax.ShapeDtypeStruct((rows, d), F32),
            grid=(rows // tr,),
            in_specs=[row_spec(tr, d), row_spec(tr, d, off_tiles), row_spec(tr, d, off_tiles),
                      mod_spec(2), vec_spec, mod_spec(5), vec_spec],
            out_specs=row_spec(tr, d),
            compiler_params=_params(1),
            name=name,
        )(x, mix, f, mods, vec(norm_mix_post), mods, vec(norm_ffn_post))

    y_prompt = final(xp, 0, "final_ctx").reshape(nbc, lc, d)
    y_sample = final(xs, tc // tr, "final_lat").reshape(nbl, ll, d)

    return y_prompt, y_sample, new_k, new_v, s_f, s_b
```
